```python
import jax
import jax.numpy as jnp
from jax import lax
import numpy as np

D_MODEL = 1024
BATCH = 2
SEQ = 16384
DEPTH = 4

N_MIXERS = 2
N_ATT_LAYERS = (DEPTH + 1) // N_MIXERS
N_HGRN_LAYERS = DEPTH // N_MIXERS
HEAD_DIM = 64
N_Q_HEADS = D_MODEL // HEAD_DIM
N_KV_HEADS = N_Q_HEADS // 4
GROUP = N_Q_HEADS // N_KV_HEADS
Q_DIM = N_Q_HEADS * HEAD_DIM
KV_DIM = N_KV_HEADS * HEAD_DIM
WINDOW = 128
ATT_BLOCK = 128
ROPE_DIM = HEAD_DIM // 4
ROPE_THETA = 500000.0
HGRN_EXPAND = 128
HGRN_HEADS = D_MODEL // HGRN_EXPAND
HGRN_KEY = HGRN_EXPAND
HGRN_VAL = D_MODEL // HGRN_HEADS
HGRN_KW = HGRN_HEADS * HGRN_KEY
HGRN_VW = HGRN_HEADS * HGRN_VAL
HGRN_IN_DIM = 3 * HGRN_KW + 2 * HGRN_VW
HGRN_CHUNK = 64
D_FF = -(-8 * D_MODEL // (3 * 256)) * 256
PLE_DIM = 256
ALPHA = (2 * DEPTH) ** 0.25
BETA = (8 * DEPTH) ** -0.25
LN_EPS = 1e-5

kernel_name = 'hybrid_swa_hgrn2_deepnorm_encoder'


def layer_norm(x, g, b):
    xf = x.astype(jnp.float32)
    mu = xf.mean(-1, keepdims=True)
    var = jnp.square(xf - mu).mean(-1, keepdims=True)
    y = (xf - mu) * lax.rsqrt(var + LN_EPS) * g.astype(jnp.float32) + b.astype(jnp.float32)
    return y.astype(x.dtype)


def rope_tables(seq):
    inv = ROPE_THETA ** (-jnp.arange(0, ROPE_DIM, 2, dtype=jnp.float32) / ROPE_DIM)
    ang = jnp.arange(seq, dtype=jnp.float32)[:, None] * inv[None, :]
    return jnp.cos(ang), jnp.sin(ang)


def apply_partial_rope(x, cos, sin):
    half = ROPE_DIM // 2
    c = cos[None, :, None, :]
    s = sin[None, :, None, :]
    xr = x[..., :ROPE_DIM].astype(jnp.float32)
    x1, x2 = xr[..., :half], xr[..., half:]
    rot = jnp.concatenate([x1 * c - x2 * s, x2 * c + x1 * s], axis=-1).astype(x.dtype)
    return jnp.concatenate([rot, x[..., ROPE_DIM:]], axis=-1)


def windowed_gqa(h, w_qkv, sink, w_o, cos, sin):
    B, S, _ = h.shape
    qkv = h @ w_qkv
    q = qkv[..., :Q_DIM].reshape(B, S, N_Q_HEADS, HEAD_DIM)
    k = qkv[..., Q_DIM:Q_DIM + KV_DIM].reshape(B, S, N_KV_HEADS, HEAD_DIM)
    v = qkv[..., Q_DIM + KV_DIM:].reshape(B, S, N_KV_HEADS, HEAD_DIM)
    q = apply_partial_rope(q, cos, sin).reshape(B, S, N_KV_HEADS, GROUP, HEAD_DIM)
    k = apply_partial_rope(k, cos, sin)
    pad = ((0, 0), (ATT_BLOCK, ATT_BLOCK), (0, 0), (0, 0))
    kp = jnp.pad(k, pad)
    vp = jnp.pad(v, pad)
    n_blocks = S // ATT_BLOCK
    scale = HEAD_DIM ** -0.5
    sink_f = sink.astype(jnp.float32).reshape(N_KV_HEADS, GROUP)[None, :, :, None, None]

    def block_attn(bi):
        start = bi * ATT_BLOCK
        qb = lax.dynamic_slice_in_dim(q, start, ATT_BLOCK, axis=1).astype(jnp.float32)
        kb = lax.dynamic_slice_in_dim(kp, start, 3 * ATT_BLOCK, axis=1).astype(jnp.float32)
        vb = lax.dynamic_slice_in_dim(vp, start, 3 * ATT_BLOCK, axis=1)
        s = jnp.einsum('bqhgd,bkhd->bhgqk', qb, kb) * scale
        qpos = start + jnp.arange(ATT_BLOCK)
        kpos = start - ATT_BLOCK + jnp.arange(3 * ATT_BLOCK)
        valid = (jnp.abs(qpos[:, None] - kpos[None, :]) <= WINDOW) & (kpos >= 0)[None, :] & (kpos < S)[None, :]
        s = jnp.where(valid, s, -jnp.inf)
        m = jnp.maximum(s.max(-1, keepdims=True), sink_f)
        pr = jnp.exp(s - m)
        den = pr.sum(-1, keepdims=True) + jnp.exp(sink_f - m)
        return jnp.einsum('bhgqk,bkhd->bqhgd', (pr / den).astype(vb.dtype), vb)

    o = lax.map(block_attn, jnp.arange(n_blocks))
    o = o.transpose(1, 0, 2, 3, 4, 5).reshape(B, S, Q_DIM)
    return o @ w_o


def gated_linear_scan(q, k, v, log_f):
    B, S, H, K = q.shape
    V = v.shape[-1]
    nc = S // HGRN_CHUNK

    def to_chunks(t):
        return t.reshape(B, nc, HGRN_CHUNK, H, t.shape[-1]).transpose(1, 0, 3, 2, 4)

    lower = jnp.tril(jnp.ones((HGRN_CHUNK, HGRN_CHUNK), dtype=bool))

    def step(state, inp):
        qc, kc, vc, gc = inp
        b = jnp.cumsum(gc, axis=2)
        inter = jnp.einsum('bhtk,bhkv->bhtv', qc * jnp.exp(b), state)
        diff = b[:, :, :, None, :] - b[:, :, None, :, :]
        decay = jnp.exp(jnp.where(lower[:, :, None], diff, -jnp.inf))
        scores = jnp.einsum('bhtk,bhsk,bhtsk->bhts', qc, kc, decay)
        intra = jnp.einsum('bhts,bhsv->bhtv', scores, vc)
        b_last = b[:, :, -1:, :]
        new_state = jnp.exp(b_last[:, :, 0, :])[..., None] * state + jnp.einsum('bhsk,bhsv->bhkv', kc * jnp.exp(b_last - b), vc)
        return new_state, inter + intra

    state0 = jnp.zeros((B, H, K, V), q.dtype)
    _, out = lax.scan(step, state0, (to_chunks(q), to_chunks(k), to_chunks(v), to_chunks(log_f)))
    return out.transpose(1, 0, 3, 2, 4).reshape(B, S, H, V)


def hgrn2_bidirectional(h, w_in, lb_fwd, lb_bwd, norm_g, w_o):
    B, S, _ = h.shape
    z = h @ w_in
    o1, o2, o3, o4 = HGRN_KW, 2 * HGRN_KW, 3 * HGRN_KW, 3 * HGRN_KW + HGRN_VW
    q = jax.nn.silu(z[..., :o1].astype(jnp.float32)).reshape(B, S, HGRN_HEADS, HGRN_KEY)
    v = z[..., o3:o4].astype(jnp.float32).reshape(B, S, HGRN_HEADS, HGRN_VAL)
    gate = z[..., o4:].astype(jnp.float32).reshape(B, S, HGRN_HEADS, HGRN_VAL)

    def forget(zf, lb):
        sig = jax.nn.sigmoid(zf.astype(jnp.float32))
        f = lb + (1.0 - lb) * sig
        k = (1.0 - lb) * (1.0 - sig)
        shape = (B, S, HGRN_HEADS, HGRN_KEY)
        return jnp.log(f).reshape(shape), k.reshape(shape)

    logf_f, k_f = forget(z[..., o1:o2], lb_fwd)
    logf_b, k_b = forget(z[..., o2:o3], lb_bwd)
    out_f = gated_linear_scan(q, k_f, v, logf_f)
    flip = lambda t: jnp.flip(t, axis=1)
    out_b = flip(gated_linear_scan(flip(q), flip(k_b), flip(v), flip(logf_b)))
    o = out_f + out_b
    o = o * lax.rsqrt(jnp.mean(o * o, axis=-1, keepdims=True) + LN_EPS) * norm_g.astype(jnp.float32)
    o = o * jax.nn.silu(gate)
    return o.reshape(B, S, HGRN_VW).astype(h.dtype) @ w_o


def swiglu(x, w_in, w_out):
    gu = x @ w_in
    g, u = jnp.split(gu, 2, axis=-1)
    return (jax.nn.silu(g) * u) @ w_out


def setup_inputs(seed: int = 0) -> dict:
    key = jax.random.key(seed)
    ks = jax.random.split(key, 17)
    f32 = jnp.float32
    D = D_MODEL

    def nrm(k, shape, scale):
        return jax.random.normal(k, shape, f32) * scale

    return {
        'x': nrm(ks[0], (BATCH, SEQ, D), 1.0),
        'p': nrm(ks[1], (DEPTH, BATCH, SEQ, PLE_DIM), 1.0),
        'att_w_qkv': nrm(ks[2], (N_ATT_LAYERS, D, Q_DIM + 2 * KV_DIM), D ** -0.5),
        'att_sink': nrm(ks[3], (N_ATT_LAYERS, N_Q_HEADS), 0.5),
        'att_w_o': nrm(ks[4], (N_ATT_LAYERS, Q_DIM, D), BETA * Q_DIM ** -0.5),
        'hgrn_w_in': nrm(ks[5], (N_HGRN_LAYERS, D, HGRN_IN_DIM), D ** -0.5),
        'hgrn_lb_logits': nrm(ks[6], (DEPTH, 2, HGRN_KW), 0.1),
        'hgrn_norm_g': 1.0 + nrm(ks[7], (N_HGRN_LAYERS, HGRN_VAL), 0.01),
        'hgrn_w_o': nrm(ks[8], (N_HGRN_LAYERS, HGRN_VW, D), BETA * HGRN_VW ** -0.5),
        'ln_mix_g': 1.0 + nrm(ks[9], (DEPTH, D), 0.01),
        'ln_mix_b': nrm(ks[10], (DEPTH, D), 0.01),
        'ffn_w_in': nrm(ks[11], (DEPTH, D, 2 * D_FF), D ** -0.5),
        'ffn_w_out': nrm(ks[12], (DEPTH, D_FF, D), BETA * D_FF ** -0.5),
        'ln_ffn_g': 1.0 + nrm(ks[13], (DEPTH, D), 0.01),
        'ln_ffn_b': nrm(ks[14], (DEPTH, D), 0.01),
        'ple_w_gate': nrm(ks[15], (DEPTH, D, D), D ** -0.5),
        'ple_w_proj': nrm(ks[16], (DEPTH, PLE_DIM, D), BETA * PLE_DIM ** -0.5),
    }


def reference(x, p, att_w_qkv, att_sink, att_w_o, hgrn_w_in, hgrn_lb_logits, hgrn_norm_g, hgrn_w_o,
              ln_mix_g, ln_mix_b, ffn_w_in, ffn_w_out, ln_ffn_g, ln_ffn_b, ple_w_gate, ple_w_proj):
    S = x.shape[1]
    cos, sin = rope_tables(S)
    lb_sm = jax.nn.softmax(hgrn_lb_logits.astype(jnp.float32), axis=0)
    lb_all = jnp.cumsum(lb_sm, axis=0) - lb_sm[0:1]
    for i in range(DEPTH):
        j = i // N_MIXERS
        if i % N_MIXERS == 0:
            mix = windowed_gqa(x, att_w_qkv[j], att_sink[j], att_w_o[j], cos, sin)
        else:
            mix = hgrn2_bidirectional(x, hgrn_w_in[j], lb_all[i, 0], lb_all[i, 1], hgrn_norm_g[j], hgrn_w_o[j])
        x = layer_norm(ALPHA * x + mix, ln_mix_g[i], ln_mix_b[i])
        x = layer_norm(ALPHA * x + swiglu(x, ffn_w_in[i], ffn_w_out[i]), ln_ffn_g[i], ln_ffn_b[i])
        x = x + jax.nn.sigmoid(x @ ple_w_gate[i]) * (p[i] @ ple_w_proj[i])
    return x
```

```python
import functools

import jax
import jax.numpy as jnp
from jax import lax
from jax.experimental import pallas as pl
from jax.experimental.pallas import tpu as pltpu

F32 = jnp.float32
BF16 = jnp.bfloat16

D_MODEL = 1024
DEPTH = 4
HEAD_DIM = 64
N_Q_HEADS = 16
N_KV_HEADS = 4
GROUP = 4
Q_DIM = 1024
KV_DIM = 256
WINDOW = 128
ROPE_DIM = 16
ROPE_THETA = 500000.0
HGRN_HEADS = 8
HGRN_KEY = 128
D_FF = 2816
PLE_DIM = 256
ALPHA = (2 * DEPTH) ** 0.25
LN_EPS = 1e-5

LANES = 128
VMEM_LIMIT = 48 * 1024 * 1024

ATT_TQ = 128
QKV_TN = 256
FFN_FC = 256
SCAN_C = 64
SCAN_BASE = 16


def _cparams(sem):
    return pltpu.CompilerParams(dimension_semantics=sem, vmem_limit_bytes=VMEM_LIMIT)


def _layer_norm(y, g, b):
    mu = jnp.mean(y, axis=-1, keepdims=True)
    d = y - mu
    var = jnp.mean(d * d, axis=-1, keepdims=True)
    return d * lax.rsqrt(var + LN_EPS) * g + b


def _sigmoid(z):
    return 1.0 / (1.0 + jnp.exp(-z))


def _qkv_kernel(x_ref, w_ref, c_ref, sa_ref, sb_ref, o_ref):
    j = pl.program_id(1)
    acc = jnp.dot(x_ref[...].astype(BF16), w_ref[...], preferred_element_type=F32)

    @pl.when(j < 5)
    def _():
        a = acc * jnp.where(j < 4, HEAD_DIM ** -0.5, 1.0)
        r = (a * c_ref[...] + pltpu.roll(a, ROPE_DIM // 2, 1) * sa_ref[...]
             + pltpu.roll(a, QKV_TN - ROPE_DIM // 2, 1) * sb_ref[...])
        o_ref[...] = r.astype(BF16)

    @pl.when(j == 5)
    def _():
        o_ref[...] = acc.astype(BF16)


def _qkv_proj(x, w, ctab, satab, sbtab, seq, tm):
    t = x.shape[0]
    ns = seq // tm
    n_out = Q_DIM + 2 * KV_DIM
    return pl.pallas_call(
        _qkv_kernel,
        grid=(t // tm, n_out // QKV_TN),
        in_specs=[
            pl.BlockSpec((tm, D_MODEL), lambda i, j: (i, 0)),
            pl.BlockSpec((D_MODEL, QKV_TN), lambda i, j: (0, j)),
            pl.BlockSpec((tm, QKV_TN), lambda i, j: (i % ns, 0)),
            pl.BlockSpec((tm, QKV_TN), lambda i, j: (i % ns, 0)),
            pl.BlockSpec((tm, QKV_TN), lambda i, j: (i % ns, 0)),
        ],
        out_specs=pl.BlockSpec((tm, QKV_TN), lambda i, j: (i, j)),
        out_shape=jax.ShapeDtypeStruct((t, n_out), BF16),
        compiler_params=_cparams(("parallel", "arbitrary")),
        name="attn_qkv",
    )(x, w, ctab, satab, sbtab)


def _attn_kernel(seq, sink_ref, q_ref, kp_ref, kc_ref, kn_ref, vp_ref, vc_ref, vn_ref, o_ref):
    i = pl.program_id(1)
    qpos = i * ATT_TQ + lax.broadcasted_iota(jnp.int32, (ATT_TQ, 3 * ATT_TQ), 0)
    kpos = (i - 1) * ATT_TQ + lax.broadcasted_iota(jnp.int32, (ATT_TQ, 3 * ATT_TQ), 1)
    valid = (jnp.abs(qpos - kpos) <= WINDOW) & (kpos >= 0) & (kpos < seq)
    for hk in range(N_KV_HEADS):
        sl = slice(hk * HEAD_DIM, (hk + 1) * HEAD_DIM)
        kh = jnp.concatenate([kp_ref[:, sl], kc_ref[:, sl], kn_ref[:, sl]], axis=0)
        vh = jnp.concatenate([vp_ref[:, sl], vc_ref[:, sl], vn_ref[:, sl]], axis=0)
        for g in range(GROUP):
            h = hk * GROUP + g
            hs = slice(h * HEAD_DIM, (h + 1) * HEAD_DIM)
            s = lax.dot_general(q_ref[:, hs], kh, (((1,), (1,)), ((), ())),
                                preferred_element_type=F32)
            s = jnp.where(valid, s, -jnp.inf)
            sink = sink_ref[h]
            m = jnp.maximum(jnp.max(s, axis=-1, keepdims=True), sink)
            p = jnp.exp(s - m)
            den = jnp.sum(p, axis=-1, keepdims=True) + jnp.exp(sink - m)
            o = jnp.dot(p.astype(BF16), vh, preferred_element_type=F32) / den
            o_ref[:, hs] = o.astype(BF16)


def _attention(qkv, sink, batch, seq):
    t = qkv.shape[0]
    nq = seq // ATT_TQ
    kcol = Q_DIM // KV_DIM
    vcol = kcol + 1

    def kv_spec(col, off):
        def imap(b, i):
            return (b * nq + jnp.clip(i + off, 0, nq - 1), col)
        return pl.BlockSpec((ATT_TQ, KV_DIM), imap)

    return pl.pallas_call(
        functools.partial(_attn_kernel, seq),
        grid=(batch, nq),
        in_specs=[
            pl.BlockSpec(memory_space=pltpu.SMEM),
            pl.BlockSpec((ATT_TQ, Q_DIM), lambda b, i: (b * nq + i, 0)),
            kv_spec(kcol, -1), kv_spec(kcol, 0), kv_spec(kcol, 1),
            kv_spec(vcol, -1), kv_spec(vcol, 0), kv_spec(vcol, 1),
        ],
        out_specs=pl.BlockSpec((ATT_TQ, Q_DIM), lambda b, i: (b * nq + i, 0)),
        out_shape=jax.ShapeDtypeStruct((t, Q_DIM), BF16),
        compiler_params=_cparams(("parallel", "arbitrary")),
        name="attn_core",
    )(sink, qkv, qkv, qkv, qkv, qkv, qkv, qkv)


def _proj_ln_kernel(x_ref, o_ref, w_ref, g_ref, b_ref, out_ref):
    y = ALPHA * x_ref[...] + jnp.dot(o_ref[...], w_ref[...], preferred_element_type=F32)
    out_ref[...] = _layer_norm(y, g_ref[...], b_ref[...])


def _proj_ln(x, o, w, g, b, tm):
    t = x.shape[0]
    row = pl.BlockSpec((tm, D_MODEL), lambda i: (i, 0))
    vec = pl.BlockSpec((1, D_MODEL), lambda i: (0, 0))
    return pl.pallas_call(
        _proj_ln_kernel,
        grid=(t // tm,),
        in_specs=[row, row, pl.BlockSpec((D_MODEL, D_MODEL), lambda i: (0, 0)), vec, vec],
        out_specs=row,
        out_shape=jax.ShapeDtypeStruct((t, D_MODEL), F32),
        compiler_params=_cparams(("parallel",)),
        name="proj_ln",
    )(x, o, w, g, b)


def _hgrn_in_kernel(layer, x_ref, w_ref, lg_ref, q_ref, kf_ref, gf_ref, kb_ref, gb_ref, v_ref, gt_ref):
    j = pl.program_id(1)
    z = jnp.dot(x_ref[...].astype(BF16), w_ref[...], preferred_element_type=F32)

    def lower_bound(d):
        rows = [lg_ref[2 * dd + d:2 * dd + d + 1, :] for dd in range(DEPTH)]
        m = functools.reduce(jnp.maximum, rows)
        e = [jnp.exp(r - m) for r in rows]
        tot = functools.reduce(jnp.add, e)
        return functools.reduce(jnp.add, e[1:layer + 1]) / tot

    def forget(d, k_ref, g_ref):
        lb = lower_bound(d)
        sig = _sigmoid(z)
        g_ref[...] = jnp.log(lb + (1.0 - lb) * sig)
        k_ref[...] = ((1.0 - lb) * (1.0 - sig)).astype(BF16)

    @pl.when(j == 0)
    def _():
        q_ref[...] = (z * _sigmoid(z)).astype(BF16)

    @pl.when(j == 1)
    def _():
        forget(0, kf_ref, gf_ref)

    @pl.when(j == 2)
    def _():
        forget(1, kb_ref, gb_ref)

    @pl.when(j == 3)
    def _():
        v_ref[...] = z.astype(BF16)

    @pl.when(j == 4)
    def _():
        gt_ref[...] = (z * _sigmoid(z)).astype(BF16)


def _hgrn_in(x, w, logits, layer, tm):
    t = x.shape[0]
    row = pl.BlockSpec((tm, D_MODEL), lambda i, j: (i, 0))
    bf = jax.ShapeDtypeStruct((t, D_MODEL), BF16)
    f32 = jax.ShapeDtypeStruct((t, D_MODEL), F32)
    return pl.pallas_call(
        functools.partial(_hgrn_in_kernel, layer),
        grid=(t // tm, 5),
        in_specs=[row, pl.BlockSpec((D_MODEL, D_MODEL), lambda i, j: (0, j)),
                  pl.BlockSpec((2 * DEPTH, D_MODEL), lambda i, j: (0, 0))],
        out_specs=[row] * 7,
        out_shape=[bf, bf, f32, bf, f32, bf, bf],
        compiler_params=_cparams(("parallel", "arbitrary")),
        name="hgrn_in",
    )(x, w, logits)


def _scan_chunk(q, k, v, g, st_ref, rev):
    c = SCAN_C
    row = lax.broadcasted_iota(jnp.int32, (c, c), 0)
    col = lax.broadcasted_iota(jnp.int32, (c, c), 1)
    tri = (col >= row) if rev else (col <= row)
    b = jnp.dot(tri.astype(F32), g, preferred_element_type=F32, precision=lax.Precision.HIGHEST)
    btot = b[0:1] if rev else b[c - 1:c]
    rowk = lax.broadcasted_iota(jnp.int32, (c, HGRN_KEY), 0)

    def ref_rows(size, offset):
        parts = [jnp.broadcast_to(b[p * size + offset:p * size + offset + 1], (size, HGRN_KEY))
                 for p in range(c // size)]
        return parts[0] if len(parts) == 1 else jnp.concatenate(parts, axis=0)

    st = st_ref[...]
    out = lax.dot_general((q * jnp.exp(b)).astype(BF16), st.astype(BF16),
                          (((1,), (1,)), ((), ())), preferred_element_type=F32)

    scores = jnp.zeros((c, c), F32)
    half = c // 2
    while half >= SCAN_BASE:
        size = 2 * half
        e = jnp.exp(-jnp.abs(b - ref_rows(size, half if rev else half - 1)))
        is_query = ((rowk % size) < half) if rev else ((rowk % size) >= half)
        qs = jnp.where(is_query, q * e, 0.0).astype(BF16)
        ks = jnp.where(is_query, 0.0, k * e).astype(BF16)
        s = lax.dot_general(qs, ks, (((1,), (1,)), ((), ())), preferred_element_type=F32)
        scores = scores + jnp.where((row // size) == (col // size), s, 0.0)
        half //= 2
    base = SCAN_BASE
    d = b - ref_rows(base, base // 2)
    s = lax.dot_general((q * jnp.exp(d)).astype(BF16), (k * jnp.exp(-d)).astype(BF16),
                        (((1,), (1,)), ((), ())), preferred_element_type=F32)
    keep = ((row // base) == (col // base)) & ((col >= row) if rev else (col <= row))
    scores = scores + jnp.where(keep, s, 0.0)

    vb = v.astype(BF16)
    out = out + jnp.dot(scores.astype(BF16), vb, preferred_element_type=F32)
    ke = (k * jnp.exp(btot - b)).astype(BF16)
    st_ref[...] = st * jnp.exp(btot) + lax.dot_general(
        vb, ke, (((0,), (0,)), ((), ())), preferred_element_type=F32)
    return out


def _scan_kernel(nchunks, qf_ref, qb_ref, vf_ref, vb_ref, kf_ref, gf_ref, kb_ref, gb_ref,
                 of_ref, ob_ref, stf_ref, stb_ref):
    @pl.when(pl.program_id(2) == 0)
    def _():
        stf_ref[...] = jnp.zeros_like(stf_ref)
        stb_ref[...] = jnp.zeros_like(stb_ref)

    def body(ci, carry):
        rf = pl.ds(pl.multiple_of(ci * SCAN_C, SCAN_C), SCAN_C)
        rb = pl.ds(pl.multiple_of((nchunks - 1 - ci) * SCAN_C, SCAN_C), SCAN_C)
        of_ref[rf, :] = _scan_chunk(qf_ref[rf, :].astype(F32), kf_ref[rf, :].astype(F32),
                                    vf_ref[rf, :].astype(F32), gf_ref[rf, :], stf_ref, False)
        ob_ref[rb, :] = _scan_chunk(qb_ref[rb, :].astype(F32), kb_ref[rb, :].astype(F32),
                                    vb_ref[rb, :].astype(F32), gb_ref[rb, :], stb_ref, True)
        return carry

    lax.fori_loop(0, nchunks, body, 0)


def _hgrn_scan(q, kf, gf, kb, gb, v, batch, seq, blk):
    t = q.shape[0]
    nb = seq // blk
    fwd = pl.BlockSpec((blk, HGRN_KEY), lambda b, h, n: (b * nb + n, h))
    bwd = pl.BlockSpec((blk, HGRN_KEY), lambda b, h, n: (b * nb + nb - 1 - n, h))
    out = jax.ShapeDtypeStruct((t, D_MODEL), F32)
    return pl.pallas_call(
        functools.partial(_scan_kernel, blk // SCAN_C),
        grid=(batch, HGRN_HEADS, nb),
        in_specs=[fwd, bwd, fwd, bwd, fwd, fwd, bwd, bwd],
        out_specs=[fwd, bwd],
        out_shape=[out, out],
        scratch_shapes=[pltpu.VMEM((HGRN_KEY, HGRN_KEY), F32), pltpu.VMEM((HGRN_KEY, HGRN_KEY), F32)],
        compiler_params=_cparams(("parallel", "parallel", "arbitrary")),
        name="hgrn_scan",
    )(q, q, v, v, kf, gf, kb, gb)


def _hgrn_out_kernel(x_ref, of_ref, ob_ref, gt_ref, ng_ref, w_ref, g_ref, b_ref, out_ref, y_ref):
    for h in range(HGRN_HEADS):
        sl = slice(h * HGRN_KEY, (h + 1) * HGRN_KEY)
        o = of_ref[:, sl] + ob_ref[:, sl]
        ms = jnp.mean(o * o, axis=-1, keepdims=True)
        y = o * lax.rsqrt(ms + LN_EPS) * ng_ref[...] * gt_ref[:, sl].astype(F32)
        y_ref[:, sl] = y.astype(BF16)
    z = ALPHA * x_ref[...] + jnp.dot(y_ref[...], w_ref[...], preferred_element_type=F32)
    out_ref[...] = _layer_norm(z, g_ref[...], b_ref[...])


def _hgrn_out(x, of, ob, gate, norm_g, w, g, b, tm):
    t = x.shape[0]
    row = pl.BlockSpec((tm, D_MODEL), lambda i: (i, 0))
    vec = pl.BlockSpec((1, D_MODEL), lambda i: (0, 0))
    return pl.pallas_call(
        _hgrn_out_kernel,
        grid=(t // tm,),
        in_specs=[row, row, row, row, pl.BlockSpec((1, HGRN_KEY), lambda i: (0, 0)),
                  pl.BlockSpec((D_MODEL, D_MODEL), lambda i: (0, 0)), vec, vec],
        out_specs=row,
        out_shape=jax.ShapeDtypeStruct((t, D_MODEL), F32),
        scratch_shapes=[pltpu.VMEM((tm, D_MODEL), BF16)],
        compiler_params=_cparams(("parallel",)),
        name="hgrn_out",
    )(x, of, ob, gate, norm_g, w, g, b)


def _ffn_kernel(x_ref, wg_ref, wu_ref, wo_ref, g_ref, b_ref, out_ref, xb_ref, acc_ref):
    j = pl.program_id(1)

    @pl.when(j == 0)
    def _():
        xb_ref[...] = x_ref[...].astype(BF16)
        acc_ref[...] = jnp.zeros_like(acc_ref)

    xb = xb_ref[...]
    gate = jnp.dot(xb, wg_ref[...], preferred_element_type=F32)
    up = jnp.dot(xb, wu_ref[...], preferred_element_type=F32)
    h = (gate * _sigmoid(gate) * up).astype(BF16)
    acc_ref[...] += jnp.dot(h, wo_ref[...], preferred_element_type=F32)

    @pl.when(j == pl.num_programs(1) - 1)
    def _():
        out_ref[...] = _layer_norm(ALPHA * x_ref[...] + acc_ref[...], g_ref[...], b_ref[...])


def _ffn(x, w_in, w_out, g, b, tm):
    t = x.shape[0]
    nf = D_FF // FFN_FC
    row = pl.BlockSpec((tm, D_MODEL), lambda i, j: (i, 0))
    vec = pl.BlockSpec((1, D_MODEL), lambda i, j: (0, 0))
    return pl.pallas_call(
        _ffn_kernel,
        grid=(t // tm, nf),
        in_specs=[row,
                  pl.BlockSpec((D_MODEL, FFN_FC), lambda i, j: (0, j)),
                  pl.BlockSpec((D_MODEL, FFN_FC), lambda i, j: (0, nf + j)),
                  pl.BlockSpec((FFN_FC, D_MODEL), lambda i, j: (j, 0)),
                  vec, vec],
        out_specs=row,
        out_shape=jax.ShapeDtypeStruct((t, D_MODEL), F32),
        scratch_shapes=[pltpu.VMEM((tm, D_MODEL), BF16), pltpu.VMEM((tm, D_MODEL), F32)],
        compiler_params=_cparams(("parallel", "arbitrary")),
        name="ffn",
    )(x, w_in, w_in, w_out, g, b)


def _ple_kernel(x_ref, p_ref, wg_ref, wp_ref, out_ref):
    x = x_ref[...]
    gate = _sigmoid(jnp.dot(x.astype(BF16), wg_ref[...], preferred_element_type=F32))
    proj = jnp.dot(p_ref[...].astype(BF16), wp_ref[...], preferred_element_type=F32)
    out_ref[...] = x + gate * proj


def _ple(x, p, w_gate, w_proj, tm):
    t = x.shape[0]
    row = pl.BlockSpec((tm, D_MODEL), lambda i: (i, 0))
    return pl.pallas_call(
        _ple_kernel,
        grid=(t // tm,),
        in_specs=[row, pl.BlockSpec((tm, PLE_DIM), lambda i: (i, 0)),
                  pl.BlockSpec((D_MODEL, D_MODEL), lambda i: (0, 0)),
                  pl.BlockSpec((PLE_DIM, D_MODEL), lambda i: (0, 0))],
        out_specs=row,
        out_shape=jax.ShapeDtypeStruct((t, D_MODEL), F32),
        compiler_params=_cparams(("parallel",)),
        name="ple",
    )(x, p, w_gate, w_proj)


def _rope_tables(seq):
    half = ROPE_DIM // 2
    inv = ROPE_THETA ** (-jnp.arange(0, ROPE_DIM, 2, dtype=F32) / ROPE_DIM)
    ang = jnp.arange(seq, dtype=F32)[:, None] * inv[None, :]
    cos, sin = jnp.cos(ang), jnp.sin(ang)
    ones = jnp.ones((seq, HEAD_DIM - ROPE_DIM), F32)
    zeros = jnp.zeros((seq, half), F32)
    rest = jnp.zeros((seq, HEAD_DIM - ROPE_DIM), F32)
    reps = QKV_TN // HEAD_DIM
    ctab = jnp.tile(jnp.concatenate([cos, cos, ones], axis=1), (1, reps))
    satab = jnp.tile(jnp.concatenate([zeros, sin, rest], axis=1), (1, reps))
    sbtab = jnp.tile(jnp.concatenate([-sin, zeros, rest], axis=1), (1, reps))
    return ctab, satab, sbtab


def _tile(n, pref):
    while n % pref:
        pref //= 2
    return pref


def kernel(x, p, att_w_qkv, att_sink, att_w_o, hgrn_w_in, hgrn_lb_logits, hgrn_norm_g, hgrn_w_o,
           ln_mix_g, ln_mix_b, ffn_w_in, ffn_w_out, ln_ffn_g, ln_ffn_b, ple_w_gate, ple_w_proj):
    batch, seq, _ = x.shape
    t = batch * seq
    assert seq % ATT_TQ == 0 and seq % SCAN_C == 0
    tm = _tile(seq, 512)
    tm_ffn = _tile(t, 1024)
    blk = _tile(seq, 512)
    ctab, satab, sbtab = _rope_tables(seq)
    logits = hgrn_lb_logits.astype(F32).reshape(2 * DEPTH, D_MODEL)
    h = x.reshape(t, D_MODEL)
    pf = p.reshape(DEPTH, t, PLE_DIM)
    vec = lambda a: a.reshape(1, -1).astype(F32)
    for i in range(DEPTH):
        j = i // 2
        if i % 2 == 0:
            qkv = _qkv_proj(h, att_w_qkv[j].astype(BF16), ctab, satab, sbtab, seq, tm)
            o = _attention(qkv, att_sink[j].astype(F32), batch, seq)
            h = _proj_ln(h, o, att_w_o[j].astype(BF16), vec(ln_mix_g[i]), vec(ln_mix_b[i]), tm)
        else:
            q, kf, gf, kb, gb, v, gate = _hgrn_in(h, hgrn_w_in[j].astype(BF16), logits, i, tm)
            of, ob = _hgrn_scan(q, kf, gf, kb, gb, v, batch, seq, blk)
            h = _hgrn_out(h, of, ob, gate, vec(hgrn_norm_g[j]), hgrn_w_o[j].astype(BF16),
                          vec(ln_mix_g[i]), vec(ln_mix_b[i]), tm)
        h = _ffn(h, ffn_w_in[i].astype(BF16), ffn_w_out[i].astype(BF16),
                 vec(ln_ffn_g[i]), vec(ln_ffn_b[i]), tm_ffn)
        h = _ple(h, pf[i], ple_w_gate[i].astype(BF16), ple_w_proj[i].astype(BF16), tm)
    return h.reshape(batch, seq, D_MODEL)
```

```python
import functools
import math

import jax
import jax.numpy as jnp
from jax import lax
from jax.experimental import pallas as pl
from jax.experimental.pallas import tpu as pltpu

F32 = jnp.float32
BF16 = jnp.bfloat16

D_MODEL = 1024
DEPTH = 4
HEAD_DIM = 64
N_Q_HEADS = 16
N_KV_HEADS = 4
GROUP = 4
Q_DIM = 1024
KV_DIM = 256
WINDOW = 128
ROPE_DIM = 16
ROPE_THETA = 500000.0
HGRN_HEADS = 8
HGRN_KEY = 128
D_FF = 2816
PLE_DIM = 256
ALPHA = (2 * DEPTH) ** 0.25
LN_EPS = 1e-5
LOG2E = math.log2(math.e)

LANES = 128
SUBLANES = 8
VMEM_LIMIT = 56 * 1024 * 1024

ROW_TILE = 512
ATT_TQ = 128
KV2_DIM = 2 * KV_DIM
ROPE_TW = 256
FFN_FC = 1408
SCAN_C = 64
SCAN_BASE = 16
SCAN_PAIR = 2 * HGRN_KEY
SCAN_SUB = 256
SCAN_BLK = 1024


def _cparams(sem):
    return pltpu.CompilerParams(dimension_semantics=sem, vmem_limit_bytes=VMEM_LIMIT)


def _resident(shape):
    return pl.BlockSpec(shape, lambda *_: (0,) * len(shape), pipeline_mode=pl.Buffered(1))


def _layer_norm(y, g, b):
    mu = jnp.mean(y, axis=-1, keepdims=True)
    d = y - mu
    var = jnp.mean(d * d, axis=-1, keepdims=True)
    return d * lax.rsqrt(var + LN_EPS) * g + b


def _sigmoid(z):
    return 0.5 * jnp.tanh(0.5 * z) + 0.5


def _qkv_kernel(x_ref, w_ref, c_ref, sa_ref, sb_ref, q_ref, k_ref, v_ref):
    xb = x_ref[...].astype(BF16)
    half = ROPE_DIM // 2

    def rope(a):
        return (a * c_ref[...] + pltpu.roll(a, half, 1) * sa_ref[...]
                + pltpu.roll(a, ROPE_TW - half, 1) * sb_ref[...])

    def proj(col):
        return jnp.dot(xb, w_ref[:, col:col + ROPE_TW], preferred_element_type=F32)

    for c in range(Q_DIM // ROPE_TW):
        q_ref[:, c * ROPE_TW:(c + 1) * ROPE_TW] = rope(proj(c * ROPE_TW) * (HEAD_DIM ** -0.5 * LOG2E)).astype(BF16)
    for c in range(KV2_DIM // ROPE_TW):
        k_ref[:, c * ROPE_TW:(c + 1) * ROPE_TW] = rope(proj(Q_DIM + c * ROPE_TW)).astype(BF16)
        v_ref[:, c * ROPE_TW:(c + 1) * ROPE_TW] = proj(Q_DIM + KV2_DIM + c * ROPE_TW).astype(BF16)


def _qkv_proj(x, w, ctab, satab, sbtab, seq, tm):
    t = x.shape[0]
    ns = seq // tm
    n_out = Q_DIM + 2 * KV2_DIM
    tab = pl.BlockSpec((tm, ROPE_TW), lambda i: (i % ns, 0))
    return pl.pallas_call(
        _qkv_kernel,
        grid=(t // tm,),
        in_specs=[pl.BlockSpec((tm, D_MODEL), lambda i: (i, 0)), _resident((D_MODEL, n_out)), tab, tab, tab],
        out_specs=[pl.BlockSpec((tm, Q_DIM), lambda i: (i, 0)),
                   pl.BlockSpec((tm, KV2_DIM), lambda i: (i, 0)),
                   pl.BlockSpec((tm, KV2_DIM), lambda i: (i, 0))],
        out_shape=[jax.ShapeDtypeStruct((t, Q_DIM), BF16),
                   jax.ShapeDtypeStruct((t, KV2_DIM), BF16),
                   jax.ShapeDtypeStruct((t, KV2_DIM), BF16)],
        compiler_params=_cparams(("parallel",)),
        name="attn_qkv",
    )(x, w, ctab, satab, sbtab)


def _attn_kernel(seq, sink_ref, q_ref, kp_ref, kc_ref, kn_ref, vp_ref, vc_ref, vn_ref, o_ref):
    i = pl.program_id(1)
    tq = ATT_TQ
    qpos = i * tq + lax.broadcasted_iota(jnp.int32, (tq, 3 * tq), 0)
    kpos = (i - 1) * tq + lax.broadcasted_iota(jnp.int32, (tq, 3 * tq), 1)
    valid = (jnp.abs(qpos - kpos) <= WINDOW) & (kpos >= 0) & (kpos < seq)
    bias = jnp.where(valid, 0.0, -jnp.inf).astype(F32)
    bias = jnp.concatenate([bias] * GROUP, axis=0)
    first = lax.broadcasted_iota(jnp.int32, (1, LANES), 1) < HEAD_DIM
    ones_a = jnp.broadcast_to(jnp.where(first, 1.0, 0.0).astype(BF16), (3 * tq, LANES))
    ones_b = jnp.broadcast_to(jnp.where(first, 0.0, 1.0).astype(BF16), (3 * tq, LANES))
    zero = jnp.zeros((), BF16)
    for hk in range(N_KV_HEADS):
        ks = slice(hk * LANES, (hk + 1) * LANES)
        k2 = jnp.concatenate([kp_ref[:, ks], kc_ref[:, ks], kn_ref[:, ks]], axis=0)
        v2 = jnp.concatenate([vp_ref[:, ks], vc_ref[:, ks], vn_ref[:, ks]], axis=0)
        va = jnp.concatenate([jnp.where(first, v2, zero), ones_a], axis=1)
        vb = jnp.concatenate([jnp.where(first, zero, v2), ones_b], axis=1)
        q0 = q_ref[:, (2 * hk) * LANES:(2 * hk + 1) * LANES]
        q1 = q_ref[:, (2 * hk + 1) * LANES:(2 * hk + 2) * LANES]
        lhs = jnp.concatenate([jnp.where(first, q0, zero), jnp.where(first, q1, zero),
                               jnp.where(first, zero, q0), jnp.where(first, zero, q1)], axis=0)
        heads = (4 * hk, 4 * hk + 2, 4 * hk + 1, 4 * hk + 3)
        sink = jnp.concatenate([jnp.full((tq, 1), sink_ref[h], F32) for h in heads], axis=0)
        s = lax.dot_general(lhs, k2, (((1,), (1,)), ((), ())), preferred_element_type=F32) + bias
        m = jnp.maximum(jnp.max(s, axis=-1, keepdims=True), sink)
        p = jnp.exp2(s - m).astype(BF16)
        es = jnp.exp2(sink - m)
        out = (jnp.dot(p[:2 * tq], va, preferred_element_type=F32)
               + jnp.dot(p[2 * tq:], vb, preferred_element_type=F32))
        den = out[:, LANES:] + jnp.where(first, es[:2 * tq], es[2 * tq:])
        o = (out[:, :LANES] / den).astype(BF16)
        o_ref[:, (2 * hk) * LANES:(2 * hk + 1) * LANES] = o[:tq]
        o_ref[:, (2 * hk + 1) * LANES:(2 * hk + 2) * LANES] = o[tq:]


def _attention(q, k2, v2, sink, batch, seq):
    t = q.shape[0]
    nq = seq // ATT_TQ

    def kv_spec(off):
        return pl.BlockSpec((ATT_TQ, KV2_DIM), lambda b, i: (b * nq + jnp.clip(i + off, 0, nq - 1), 0))

    return pl.pallas_call(
        functools.partial(_attn_kernel, seq),
        grid=(batch, nq),
        in_specs=[
            pl.BlockSpec(memory_space=pltpu.SMEM),
            pl.BlockSpec((ATT_TQ, Q_DIM), lambda b, i: (b * nq + i, 0)),
            kv_spec(-1), kv_spec(0), kv_spec(1), kv_spec(-1), kv_spec(0), kv_spec(1),
        ],
        out_specs=pl.BlockSpec((ATT_TQ, Q_DIM), lambda b, i: (b * nq + i, 0)),
        out_shape=jax.ShapeDtypeStruct((t, Q_DIM), BF16),
        compiler_params=_cparams(("parallel", "arbitrary")),
        name="attn_core",
    )(sink, q, k2, k2, k2, v2, v2, v2)


def _proj_ln_kernel(x_ref, o_ref, w_ref, g_ref, b_ref, out_ref):
    y = ALPHA * x_ref[...] + jnp.dot(o_ref[...], w_ref[...], preferred_element_type=F32)
    out_ref[...] = _layer_norm(y, g_ref[...], b_ref[...])


def _proj_ln(x, o, w, g, b, tm):
    t = x.shape[0]
    row = pl.BlockSpec((tm, D_MODEL), lambda i: (i, 0))
    return pl.pallas_call(
        _proj_ln_kernel,
        grid=(t // tm,),
        in_specs=[row, row, _resident((D_MODEL, D_MODEL)), _resident((1, D_MODEL)), _resident((1, D_MODEL))],
        out_specs=row,
        out_shape=jax.ShapeDtypeStruct((t, D_MODEL), F32),
        compiler_params=_cparams(("parallel",)),
        name="proj_ln",
    )(x, o, w, g, b)


def _chunk_cumsum(g, rev):
    c, w = g.shape
    sub = lax.broadcasted_iota(jnp.int32, (c, w), 0) % SUBLANES
    sh = 1
    while sh < SUBLANES:
        if rev:
            g = g + jnp.where(sub < SUBLANES - sh, pltpu.roll(g, c - sh, 0), 0.0)
        else:
            g = g + jnp.where(sub >= sh, pltpu.roll(g, sh, 0), 0.0)
        sh *= 2
    groups = [g[i * SUBLANES:(i + 1) * SUBLANES] for i in range(c // SUBLANES)]
    order = range(len(groups) - 1, -1, -1) if rev else range(len(groups))
    run = None
    out = [None] * len(groups)
    for i in order:
        gi = groups[i] if run is None else groups[i] + run
        out[i] = gi
        run = jnp.broadcast_to(gi[0:1] if rev else gi[SUBLANES - 1:SUBLANES], (SUBLANES, w))
    return jnp.concatenate(out, axis=0)


def _hgrn_in_kernel(layer, x_ref, w_ref, lg_ref, q_ref, kf_ref, bf_ref, kb_ref, bb_ref, v_ref, gt_ref):
    xb = x_ref[...].astype(BF16)
    tm = xb.shape[0]

    def proj(sec):
        return jnp.dot(xb, w_ref[:, sec * D_MODEL:(sec + 1) * D_MODEL], preferred_element_type=F32)

    def silu(z):
        return z * _sigmoid(z)

    def lower_bound(d):
        rows = [lg_ref[2 * dd + d:2 * dd + d + 1, :] for dd in range(DEPTH)]
        m = functools.reduce(jnp.maximum, rows)
        e = [jnp.exp(r - m) for r in rows]
        return functools.reduce(jnp.add, e[1:layer + 1]) / functools.reduce(jnp.add, e)

    def forget(sec, d, k_ref, b_ref):
        lb = lower_bound(d)
        c = 0.5 * (1.0 - lb)
        t = c * jnp.tanh(0.5 * proj(sec))
        k_ref[...] = (c - t).astype(BF16)
        g = jnp.log2((0.5 * (1.0 + lb)) + t)
        for ci in range(tm // SCAN_C):
            rows = slice(ci * SCAN_C, (ci + 1) * SCAN_C)
            b_ref[rows, :] = _chunk_cumsum(g[rows], d == 1)

    q_ref[...] = silu(proj(0)).astype(BF16)
    forget(1, 0, kf_ref, bf_ref)
    forget(2, 1, kb_ref, bb_ref)
    v_ref[...] = proj(3).astype(BF16)
    gt_ref[...] = silu(proj(4)).astype(BF16)


def _hgrn_in(x, w, logits, layer, tm):
    t = x.shape[0]
    row = pl.BlockSpec((tm, D_MODEL), lambda i: (i, 0))
    bf = jax.ShapeDtypeStruct((t, D_MODEL), BF16)
    f32 = jax.ShapeDtypeStruct((t, D_MODEL), F32)
    return pl.pallas_call(
        functools.partial(_hgrn_in_kernel, layer),
        grid=(t // tm,),
        in_specs=[row, _resident((D_MODEL, 5 * D_MODEL)), _resident((2 * DEPTH, D_MODEL))],
        out_specs=[row] * 7,
        out_shape=[bf, bf, f32, bf, f32, bf, bf],
        compiler_params=_cparams(("parallel",)),
        name="hgrn_in",
    )(x, w, logits)


def _block_diag(a0, a1):
    z = jnp.zeros_like(a0)
    return jnp.concatenate([jnp.concatenate([a0, z], axis=1), jnp.concatenate([z, a1], axis=1)], axis=0)


def _scan_consts(rev):
    c = SCAN_C
    row = lax.broadcasted_iota(jnp.int32, (c, 2 * c), 0)
    col = lax.broadcasted_iota(jnp.int32, (c, 2 * c), 1) % c
    rowk = lax.broadcasted_iota(jnp.int32, (c, SCAN_PAIR), 0)
    masks, signs = {}, {}
    half = c // 2
    while half >= SCAN_BASE:
        size = 2 * half
        q_row = ((row % size) < half) if rev else ((row % size) >= half)
        k_col = ((col % size) >= half) if rev else ((col % size) < half)
        masks[size] = ((row // size) == (col // size)) & q_row & k_col
        q_rowk = ((rowk % size) < half) if rev else ((rowk % size) >= half)
        signs[size] = jnp.where(q_rowk, 1.0, -1.0).astype(F32)
        half //= 2
    masks[0] = ((row // SCAN_BASE) == (col // SCAN_BASE)) & ((col >= row) if rev else (col <= row))
    return masks, signs


def _scan_chunk_local(q, k, v, b, rev, masks, signs):
    c = SCAN_C
    hk = HGRN_KEY
    btot = b[0:1] if rev else b[c - 1:c]

    def ref_rows(size, offset):
        parts = [jnp.broadcast_to(b[p * size + offset:p * size + offset + 1], (size, SCAN_PAIR))
                 for p in range(c // size)]
        return parts[0] if len(parts) == 1 else jnp.concatenate(parts, axis=0)

    def pair_scores(qs, ks):
        kbd = _block_diag(ks[:, :hk], ks[:, hk:])
        return lax.dot_general(qs, kbd, (((1,), (1,)), ((), ())), preferred_element_type=F32)

    scores = None
    half = c // 2
    while half >= SCAN_BASE:
        size = 2 * half
        e = jnp.exp2((b - ref_rows(size, half if rev else half - 1)) * signs[size]).astype(BF16)
        s = jnp.where(masks[size], pair_scores(q * e, k * e), 0.0)
        scores = s if scores is None else scores + s
        half //= 2
    d = b - ref_rows(SCAN_BASE, SCAN_BASE // 2)
    s = pair_scores(q * jnp.exp2(d).astype(BF16), k * jnp.exp2(-d).astype(BF16))
    scores = scores + jnp.where(masks[0], s, 0.0)

    ke = k * jnp.exp2(btot - b).astype(BF16)
    contrib = [lax.dot_general(v[:, sl], ke[:, sl], (((0,), (0,)), ((), ())), preferred_element_type=F32)
               for sl in (slice(0, hk), slice(hk, 2 * hk))]
    return q * jnp.exp2(b).astype(BF16), scores.astype(BF16), contrib, jnp.exp2(btot)


def _scan_direction(q_ref, k_ref, v_ref, b_ref, o_ref, st_ref, rev, base, nchunks):
    hk = HGRN_KEY
    masks, signs = _scan_consts(rev)
    rows = [pl.ds(base + ci * SCAN_C, SCAN_C) for ci in range(nchunks)]
    local = [_scan_chunk_local(q_ref[r, :], k_ref[r, :], v_ref[r, :], b_ref[r, :], rev, masks, signs)
             for r in rows]
    st = [st_ref[0], st_ref[1]]
    for ci in (range(nchunks - 1, -1, -1) if rev else range(nchunks)):
        qe, scores, contrib, dec = local[ci]
        v = v_ref[rows[ci], :]
        w = _block_diag(st[0].astype(BF16), st[1].astype(BF16))
        o_ref[rows[ci], :] = (
            jnp.dot(scores, _block_diag(v[:, :hk], v[:, hk:]), preferred_element_type=F32)
            + lax.dot_general(qe, w, (((1,), (1,)), ((), ())), preferred_element_type=F32))
        st = [st[0] * dec[:, :hk] + contrib[0], st[1] * dec[:, hk:] + contrib[1]]
    st_ref[0] = st[0]
    st_ref[1] = st[1]


def _scan_kernel(nsub, sub, qf_ref, qb_ref, vf_ref, vb_ref, kf_ref, bf_ref, kb_ref, bb_ref,
                 of_ref, ob_ref, stf_ref, stb_ref):
    @pl.when(pl.program_id(2) == 0)
    def _():
        stf_ref[...] = jnp.zeros_like(stf_ref)
        stb_ref[...] = jnp.zeros_like(stb_ref)

    nchunks = sub // SCAN_C

    def body(i, carry):
        fbase = pl.multiple_of(i * sub, sub)
        bbase = pl.multiple_of((nsub - 1 - i) * sub, sub)
        _scan_direction(qf_ref, kf_ref, vf_ref, bf_ref, of_ref, stf_ref, False, fbase, nchunks)
        _scan_direction(qb_ref, kb_ref, vb_ref, bb_ref, ob_ref, stb_ref, True, bbase, nchunks)
        return carry

    lax.fori_loop(0, nsub, body, 0)


def _hgrn_scan(q, kf, bf, kb, bb, v, batch, seq, blk, sub):
    t = q.shape[0]
    nb = seq // blk
    fwd = pl.BlockSpec((blk, SCAN_PAIR), lambda b, h, n: (b * nb + n, h))
    bwd = pl.BlockSpec((blk, SCAN_PAIR), lambda b, h, n: (b * nb + nb - 1 - n, h))
    out = jax.ShapeDtypeStruct((t, D_MODEL), F32)
    state = pltpu.VMEM((2, HGRN_KEY, HGRN_KEY), F32)
    return pl.pallas_call(
        functools.partial(_scan_kernel, blk // sub, sub),
        grid=(batch, D_MODEL // SCAN_PAIR, nb),
        in_specs=[fwd, bwd, fwd, bwd, fwd, fwd, bwd, bwd],
        out_specs=[fwd, bwd],
        out_shape=[out, out],
        scratch_shapes=[state, state],
        compiler_params=_cparams(("parallel", "parallel", "arbitrary")),
        name="hgrn_scan",
    )(q, q, v, v, kf, bf, kb, bb)


def _hgrn_out_kernel(x_ref, of_ref, ob_ref, gt_ref, ng_ref, w_ref, g_ref, b_ref, out_ref, y_ref):
    for h in range(HGRN_HEADS):
        sl = slice(h * HGRN_KEY, (h + 1) * HGRN_KEY)
        o = of_ref[:, sl] + ob_ref[:, sl]
        ms = jnp.mean(o * o, axis=-1, keepdims=True)
        y = o * lax.rsqrt(ms + LN_EPS) * ng_ref[...] * gt_ref[:, sl].astype(F32)
        y_ref[:, sl] = y.astype(BF16)
    z = ALPHA * x_ref[...] + jnp.dot(y_ref[...], w_ref[...], preferred_element_type=F32)
    out_ref[...] = _layer_norm(z, g_ref[...], b_ref[...])


def _hgrn_out(x, of, ob, gate, norm_g, w, g, b, tm):
    t = x.shape[0]
    row = pl.BlockSpec((tm, D_MODEL), lambda i: (i, 0))
    return pl.pallas_call(
        _hgrn_out_kernel,
        grid=(t // tm,),
        in_specs=[row, row, row, row, _resident((1, HGRN_KEY)), _resident((D_MODEL, D_MODEL)),
                  _resident((1, D_MODEL)), _resident((1, D_MODEL))],
        out_specs=row,
        out_shape=jax.ShapeDtypeStruct((t, D_MODEL), F32),
        scratch_shapes=[pltpu.VMEM((tm, D_MODEL), BF16)],
        compiler_params=_cparams(("parallel",)),
        name="hgrn_out",
    )(x, of, ob, gate, norm_g, w, g, b)


def _ffn_kernel(x_ref, p_ref, wi_ref, wo_ref, g_ref, b_ref, wg_ref, wp_ref, out_ref):
    x = x_ref[...]
    xb = x.astype(BF16)
    acc = None
    for c in range(D_FF // FFN_FC):
        lo = c * FFN_FC
        gate = jnp.dot(xb, wi_ref[:, lo:lo + FFN_FC], preferred_element_type=F32)
        up = jnp.dot(xb, wi_ref[:, D_FF + lo:D_FF + lo + FFN_FC], preferred_element_type=F32)
        h = (gate * _sigmoid(gate) * up).astype(BF16)
        y = jnp.dot(h, wo_ref[lo:lo + FFN_FC, :], preferred_element_type=F32)
        acc = y if acc is None else acc + y
    x2 = _layer_norm(ALPHA * x + acc, g_ref[...], b_ref[...])
    egate = _sigmoid(jnp.dot(x2.astype(BF16), wg_ref[...], preferred_element_type=F32))
    proj = jnp.dot(p_ref[...].astype(BF16), wp_ref[...], preferred_element_type=F32)
    out_ref[...] = x2 + egate * proj


def _ffn_ple(x, p, w_in, w_out, g, b, w_gate, w_proj, tm):
    t = x.shape[0]
    row = pl.BlockSpec((tm, D_MODEL), lambda i: (i, 0))
    return pl.pallas_call(
        _ffn_kernel,
        grid=(t // tm,),
        in_specs=[row, pl.BlockSpec((tm, PLE_DIM), lambda i: (i, 0)),
                  _resident((D_MODEL, 2 * D_FF)), _resident((D_FF, D_MODEL)),
                  _resident((1, D_MODEL)), _resident((1, D_MODEL)),
                  _resident((D_MODEL, D_MODEL)), _resident((PLE_DIM, D_MODEL))],
        out_specs=row,
        out_shape=jax.ShapeDtypeStruct((t, D_MODEL), F32),
        compiler_params=_cparams(("parallel",)),
        name="ffn_ple",
    )(x, p, w_in, w_out, g, b, w_gate, w_proj)


def _rope_tables(seq):
    half = ROPE_DIM // 2
    inv = ROPE_THETA ** (-jnp.arange(0, ROPE_DIM, 2, dtype=F32) / ROPE_DIM)
    ang = jnp.arange(seq, dtype=F32)[:, None] * inv[None, :]
    cos, sin = jnp.cos(ang), jnp.sin(ang)
    ones = jnp.ones((seq, HEAD_DIM - ROPE_DIM), F32)
    zeros = jnp.zeros((seq, half), F32)
    rest = jnp.zeros((seq, HEAD_DIM - ROPE_DIM), F32)
    reps = ROPE_TW // HEAD_DIM
    ctab = jnp.tile(jnp.concatenate([cos, cos, ones], axis=1), (1, reps))
    satab = jnp.tile(jnp.concatenate([zeros, sin, rest], axis=1), (1, reps))
    sbtab = jnp.tile(jnp.concatenate([-sin, zeros, rest], axis=1), (1, reps))
    return ctab, satab, sbtab


def _twice_per_head(w):
    d = w.shape[0]
    w = w.reshape(d, N_KV_HEADS, 1, HEAD_DIM)
    return jnp.broadcast_to(w, (d, N_KV_HEADS, 2, HEAD_DIM)).reshape(d, KV2_DIM)


def _tile(n, pref):
    while n % pref:
        pref //= 2
    return pref


def kernel(x, p, att_w_qkv, att_sink, att_w_o, hgrn_w_in, hgrn_lb_logits, hgrn_norm_g, hgrn_w_o,
           ln_mix_g, ln_mix_b, ffn_w_in, ffn_w_out, ln_ffn_g, ln_ffn_b, ple_w_gate, ple_w_proj):
    batch, seq, _ = x.shape
    t = batch * seq
    assert seq % ATT_TQ == 0 and seq % SCAN_SUB == 0
    tm = _tile(seq, ROW_TILE)
    blk = _tile(seq, SCAN_BLK)
    ctab, satab, sbtab = _rope_tables(seq)
    logits = hgrn_lb_logits.astype(F32).reshape(2 * DEPTH, D_MODEL)
    h = x.reshape(t, D_MODEL)
    pf = p.reshape(DEPTH, t, PLE_DIM)
    vec = lambda a: a.reshape(1, -1).astype(F32)
    for i in range(DEPTH):
        j = i // 2
        if i % 2 == 0:
            wq, wk, wv = (att_w_qkv[j][:, :Q_DIM], att_w_qkv[j][:, Q_DIM:Q_DIM + KV_DIM],
                          att_w_qkv[j][:, Q_DIM + KV_DIM:])
            w = jnp.concatenate([wq, _twice_per_head(wk), _twice_per_head(wv)], axis=1).astype(BF16)
            q, k2, v2 = _qkv_proj(h, w, ctab, satab, sbtab, seq, tm)
            o = _attention(q, k2, v2, att_sink[j].astype(F32) * LOG2E, batch, seq)
            h = _proj_ln(h, o, att_w_o[j].astype(BF16), vec(ln_mix_g[i]), vec(ln_mix_b[i]), tm)
        else:
            q, kf, bf, kb, bb, v, gate = _hgrn_in(h, hgrn_w_in[j].astype(BF16), logits, i, tm)
            of, ob = _hgrn_scan(q, kf, bf, kb, bb, v, batch, seq, blk, SCAN_SUB)
            h = _hgrn_out(h, of, ob, gate, vec(hgrn_norm_g[j]), hgrn_w_o[j].astype(BF16),
                          vec(ln_mix_g[i]), vec(ln_mix_b[i]), tm)
        h = _ffn_ple(h, pf[i], ffn_w_in[i].astype(BF16), ffn_w_out[i].astype(BF16),
                     vec(ln_ffn_g[i]), vec(ln_ffn_b[i]),
                     ple_w_gate[i].astype(BF16), ple_w_proj[i].astype(BF16), tm)
    return h.reshape(batch, seq, D_MODEL)
```

```python
import functools
import math

import jax
import jax.numpy as jnp
from jax import lax
from jax.experimental import pallas as pl
from jax.experimental.pallas import tpu as pltpu

F32 = jnp.float32
BF16 = jnp.bfloat16

D_MODEL = 1024
DEPTH = 4
HEAD_DIM = 64
N_Q_HEADS = 16
N_KV_HEADS = 4
GROUP = 4
Q_DIM = 1024
KV_DIM = 256
WINDOW = 128
ROPE_DIM = 16
ROPE_THETA = 500000.0
HGRN_HEADS = 8
HGRN_KEY = 128
D_FF = 2816
PLE_DIM = 256
ALPHA = (2 * DEPTH) ** 0.25
LN_EPS = 1e-5
LOG2E = math.log2(math.e)

LANES = 128
SUBLANES = 8
VMEM_LIMIT = 56 * 1024 * 1024

ROW_TILE = 512
ATT_TQ = 128
KV2_DIM = 2 * KV_DIM
ROPE_TW = 256
FFN_FC = 1408
SCAN_C = 64
SCAN_BASE = 16
SCAN_PAIR = 2 * HGRN_KEY
SCAN_SUB = 1024
SCAN_BLK = 1024
HGRN_IN_TN = 256
HGRN_IN_TR = 256


def _cparams(sem):
    return pltpu.CompilerParams(dimension_semantics=sem, vmem_limit_bytes=VMEM_LIMIT)


def _resident(shape):
    return pl.BlockSpec(shape, lambda *_: (0,) * len(shape), pipeline_mode=pl.Buffered(1))


def _layer_norm(y, g, b):
    mu = jnp.mean(y, axis=-1, keepdims=True)
    d = y - mu
    var = jnp.mean(d * d, axis=-1, keepdims=True)
    return d * lax.rsqrt(var + LN_EPS) * g + b


def _sigmoid(z):
    return 0.5 * jnp.tanh(0.5 * z) + 0.5


def _qkv_kernel(x_ref, w_ref, c_ref, sa_ref, sb_ref, q_ref, k_ref, v_ref):
    xb = x_ref[...].astype(BF16)
    half = ROPE_DIM // 2

    def rope(a):
        return (a * c_ref[...] + pltpu.roll(a, half, 1) * sa_ref[...]
                + pltpu.roll(a, ROPE_TW - half, 1) * sb_ref[...])

    def proj(col):
        return jnp.dot(xb, w_ref[:, col:col + ROPE_TW], preferred_element_type=F32)

    for c in range(Q_DIM // ROPE_TW):
        q_ref[:, c * ROPE_TW:(c + 1) * ROPE_TW] = rope(proj(c * ROPE_TW) * (HEAD_DIM ** -0.5 * LOG2E)).astype(BF16)
    for c in range(KV2_DIM // ROPE_TW):
        k_ref[:, c * ROPE_TW:(c + 1) * ROPE_TW] = rope(proj(Q_DIM + c * ROPE_TW)).astype(BF16)
        v_ref[:, c * ROPE_TW:(c + 1) * ROPE_TW] = proj(Q_DIM + KV2_DIM + c * ROPE_TW).astype(BF16)


def _qkv_proj(x, w, ctab, satab, sbtab, seq, tm):
    t = x.shape[0]
    ns = seq // tm
    n_out = Q_DIM + 2 * KV2_DIM
    tab = pl.BlockSpec((tm, ROPE_TW), lambda i: (i % ns, 0))
    return pl.pallas_call(
        _qkv_kernel,
        grid=(t // tm,),
        in_specs=[pl.BlockSpec((tm, D_MODEL), lambda i: (i, 0)), _resident((D_MODEL, n_out)), tab, tab, tab],
        out_specs=[pl.BlockSpec((tm, Q_DIM), lambda i: (i, 0)),
                   pl.BlockSpec((tm, KV2_DIM), lambda i: (i, 0)),
                   pl.BlockSpec((tm, KV2_DIM), lambda i: (i, 0))],
        out_shape=[jax.ShapeDtypeStruct((t, Q_DIM), BF16),
                   jax.ShapeDtypeStruct((t, KV2_DIM), BF16),
                   jax.ShapeDtypeStruct((t, KV2_DIM), BF16)],
        compiler_params=_cparams(("parallel",)),
        name="attn_qkv",
    )(x, w, ctab, satab, sbtab)


def _attn_kernel(seq, sink_ref, q_ref, kp_ref, kc_ref, kn_ref, vp_ref, vc_ref, vn_ref, o_ref):
    i = pl.program_id(1)
    tq = ATT_TQ
    qpos = i * tq + lax.broadcasted_iota(jnp.int32, (tq, 3 * tq), 0)
    kpos = (i - 1) * tq + lax.broadcasted_iota(jnp.int32, (tq, 3 * tq), 1)
    valid = (jnp.abs(qpos - kpos) <= WINDOW) & (kpos >= 0) & (kpos < seq)
    bias = jnp.where(valid, 0.0, -jnp.inf).astype(F32)
    bias = jnp.concatenate([bias] * GROUP, axis=0)
    first = lax.broadcasted_iota(jnp.int32, (1, LANES), 1) < HEAD_DIM
    ones_a = jnp.broadcast_to(jnp.where(first, 1.0, 0.0).astype(BF16), (3 * tq, LANES))
    ones_b = jnp.broadcast_to(jnp.where(first, 0.0, 1.0).astype(BF16), (3 * tq, LANES))
    zero = jnp.zeros((), BF16)
    for hk in range(N_KV_HEADS):
        ks = slice(hk * LANES, (hk + 1) * LANES)
        k2 = jnp.concatenate([kp_ref[:, ks], kc_ref[:, ks], kn_ref[:, ks]], axis=0)
        v2 = jnp.concatenate([vp_ref[:, ks], vc_ref[:, ks], vn_ref[:, ks]], axis=0)
        va = jnp.concatenate([jnp.where(first, v2, zero), ones_a], axis=1)
        vb = jnp.concatenate([jnp.where(first, zero, v2), ones_b], axis=1)
        q0 = q_ref[:, (2 * hk) * LANES:(2 * hk + 1) * LANES]
        q1 = q_ref[:, (2 * hk + 1) * LANES:(2 * hk + 2) * LANES]
        lhs = jnp.concatenate([jnp.where(first, q0, zero), jnp.where(first, q1, zero),
                               jnp.where(first, zero, q0), jnp.where(first, zero, q1)], axis=0)
        heads = (4 * hk, 4 * hk + 2, 4 * hk + 1, 4 * hk + 3)
        sink = jnp.concatenate([jnp.full((tq, 1), sink_ref[h], F32) for h in heads], axis=0)
        s = lax.dot_general(lhs, k2, (((1,), (1,)), ((), ())), preferred_element_type=F32) + bias
        m = jnp.maximum(jnp.max(s, axis=-1, keepdims=True), sink)
        p = jnp.exp2(s - m).astype(BF16)
        es = jnp.exp2(sink - m)
        out = (jnp.dot(p[:2 * tq], va, preferred_element_type=F32)
               + jnp.dot(p[2 * tq:], vb, preferred_element_type=F32))
        den = out[:, LANES:] + jnp.where(first, es[:2 * tq], es[2 * tq:])
        o = (out[:, :LANES] / den).astype(BF16)
        o_ref[:, (2 * hk) * LANES:(2 * hk + 1) * LANES] = o[:tq]
        o_ref[:, (2 * hk + 1) * LANES:(2 * hk + 2) * LANES] = o[tq:]


def _attention(q, k2, v2, sink, batch, seq):
    t = q.shape[0]
    nq = seq // ATT_TQ

    def kv_spec(off):
        return pl.BlockSpec((ATT_TQ, KV2_DIM), lambda b, i: (b * nq + jnp.clip(i + off, 0, nq - 1), 0))

    return pl.pallas_call(
        functools.partial(_attn_kernel, seq),
        grid=(batch, nq),
        in_specs=[
            pl.BlockSpec(memory_space=pltpu.SMEM),
            pl.BlockSpec((ATT_TQ, Q_DIM), lambda b, i: (b * nq + i, 0)),
            kv_spec(-1), kv_spec(0), kv_spec(1), kv_spec(-1), kv_spec(0), kv_spec(1),
        ],
        out_specs=pl.BlockSpec((ATT_TQ, Q_DIM), lambda b, i: (b * nq + i, 0)),
        out_shape=jax.ShapeDtypeStruct((t, Q_DIM), BF16),
        compiler_params=_cparams(("parallel", "arbitrary")),
        name="attn_core",
    )(sink, q, k2, k2, k2, v2, v2, v2)


def _chunk_cumsum(g, rev):
    c, w = g.shape
    sub = lax.broadcasted_iota(jnp.int32, (c, w), 0) % SUBLANES
    sh = 1
    while sh < SUBLANES:
        if rev:
            g = g + jnp.where(sub < SUBLANES - sh, pltpu.roll(g, c - sh, 0), 0.0)
        else:
            g = g + jnp.where(sub >= sh, pltpu.roll(g, sh, 0), 0.0)
        sh *= 2
    groups = [g[i * SUBLANES:(i + 1) * SUBLANES] for i in range(c // SUBLANES)]
    order = range(len(groups) - 1, -1, -1) if rev else range(len(groups))
    run = None
    out = [None] * len(groups)
    for i in order:
        gi = groups[i] if run is None else groups[i] + run
        out[i] = gi
        run = jnp.broadcast_to(gi[0:1] if rev else gi[SUBLANES - 1:SUBLANES], (SUBLANES, w))
    return jnp.concatenate(out, axis=0)


def _hgrn_in_kernel(layer, x_ref, w_ref, lg_ref, q_ref, kf_ref, bf_ref, kb_ref, bb_ref, v_ref, gt_ref):
    for r0 in range(0, x_ref.shape[0], HGRN_IN_TR):
        _hgrn_in_rows(layer, slice(r0, r0 + HGRN_IN_TR), x_ref, w_ref, lg_ref,
                      q_ref, kf_ref, bf_ref, kb_ref, bb_ref, v_ref, gt_ref)


def _hgrn_in_rows(layer, rws, x_ref, w_ref, lg_ref, q_ref, kf_ref, bf_ref, kb_ref, bb_ref, v_ref, gt_ref):
    xb = x_ref[rws, :].astype(BF16)
    tm = xb.shape[0]
    tn = HGRN_IN_TN

    def proj(sec, cols):
        lo = sec * D_MODEL + cols.start
        return jnp.dot(xb, w_ref[:, lo:lo + tn], preferred_element_type=F32)

    def silu(z):
        return z * _sigmoid(z)

    def lower_bound(d, cols):
        rows = [lg_ref[2 * dd + d:2 * dd + d + 1, cols] for dd in range(DEPTH)]
        m = functools.reduce(jnp.maximum, rows)
        e = [jnp.exp(r - m) for r in rows]
        return functools.reduce(jnp.add, e[1:layer + 1]) / functools.reduce(jnp.add, e)

    def forget(sec, d, cols, k_ref, b_ref):
        lb = lower_bound(d, cols)
        c = 0.5 * (1.0 - lb)
        t = c * jnp.tanh(0.5 * proj(sec, cols))
        k_ref[rws, cols] = (c - t).astype(BF16)
        g = jnp.log2((0.5 * (1.0 + lb)) + t)
        for ci in range(tm // SCAN_C):
            rows = slice(ci * SCAN_C, (ci + 1) * SCAN_C)
            b_ref[rws.start + rows.start:rws.start + rows.stop, cols] = _chunk_cumsum(g[rows], d == 1)

    for cc in range(D_MODEL // tn):
        cols = slice(cc * tn, (cc + 1) * tn)
        q_ref[rws, cols] = silu(proj(0, cols)).astype(BF16)
        forget(1, 0, cols, kf_ref, bf_ref)
        forget(2, 1, cols, kb_ref, bb_ref)
        v_ref[rws, cols] = proj(3, cols).astype(BF16)
        gt_ref[rws, cols] = silu(proj(4, cols)).astype(BF16)


def _hgrn_in(x, w, logits, layer, tm):
    t = x.shape[0]
    row = pl.BlockSpec((tm, D_MODEL), lambda i: (i, 0))
    bf = jax.ShapeDtypeStruct((t, D_MODEL), BF16)
    f32 = jax.ShapeDtypeStruct((t, D_MODEL), F32)
    return pl.pallas_call(
        functools.partial(_hgrn_in_kernel, layer),
        grid=(t // tm,),
        in_specs=[row, _resident((D_MODEL, 5 * D_MODEL)), _resident((2 * DEPTH, D_MODEL))],
        out_specs=[row] * 7,
        out_shape=[bf, bf, f32, bf, f32, bf, bf],
        compiler_params=_cparams(("parallel",)),
        name="hgrn_in",
    )(x, w, logits)


def _block_diag(a0, a1):
    z = jnp.zeros_like(a0)
    return jnp.concatenate([jnp.concatenate([a0, z], axis=1), jnp.concatenate([z, a1], axis=1)], axis=0)


def _scan_consts(rev):
    c = SCAN_C
    row = lax.broadcasted_iota(jnp.int32, (c, 2 * c), 0)
    col = lax.broadcasted_iota(jnp.int32, (c, 2 * c), 1) % c
    rowk = lax.broadcasted_iota(jnp.int32, (c, SCAN_PAIR), 0)
    masks, signs = {}, {}
    half = c // 2
    while half >= SCAN_BASE:
        size = 2 * half
        q_row = ((row % size) < half) if rev else ((row % size) >= half)
        k_col = ((col % size) >= half) if rev else ((col % size) < half)
        masks[size] = ((row // size) == (col // size)) & q_row & k_col
        q_rowk = ((rowk % size) < half) if rev else ((rowk % size) >= half)
        signs[size] = jnp.where(q_rowk, 1.0, -1.0).astype(F32)
        half //= 2
    masks[0] = ((row // SCAN_BASE) == (col // SCAN_BASE)) & ((col >= row) if rev else (col <= row))
    return masks, signs


def _scan_chunk_local(q, k, v, b, rev, masks, signs):
    c = SCAN_C
    hk = HGRN_KEY
    btot = b[0:1] if rev else b[c - 1:c]

    def ref_rows(size, offset):
        parts = [jnp.broadcast_to(b[p * size + offset:p * size + offset + 1], (size, SCAN_PAIR))
                 for p in range(c // size)]
        return parts[0] if len(parts) == 1 else jnp.concatenate(parts, axis=0)

    def pair_scores(qs, ks):
        kbd = _block_diag(ks[:, :hk], ks[:, hk:])
        return lax.dot_general(qs, kbd, (((1,), (1,)), ((), ())), preferred_element_type=F32)

    scores = None
    half = c // 2
    while half >= SCAN_BASE:
        size = 2 * half
        e = jnp.exp2((b - ref_rows(size, half if rev else half - 1)) * signs[size]).astype(BF16)
        s = jnp.where(masks[size], pair_scores(q * e, k * e), 0.0)
        scores = s if scores is None else scores + s
        half //= 2
    d = b - ref_rows(SCAN_BASE, SCAN_BASE // 2)
    s = pair_scores(q * jnp.exp2(d).astype(BF16), k * jnp.exp2(-d).astype(BF16))
    scores = scores + jnp.where(masks[0], s, 0.0)

    ke = k * jnp.exp2(btot - b).astype(BF16)
    contrib = [lax.dot_general(v[:, sl], ke[:, sl], (((0,), (0,)), ((), ())), preferred_element_type=F32)
               for sl in (slice(0, hk), slice(hk, 2 * hk))]
    return q * jnp.exp2(b).astype(BF16), scores.astype(BF16), contrib, jnp.exp2(btot)


def _scan_direction(q_ref, k_ref, v_ref, b_ref, o_ref, st_ref, rev, base, nchunks):
    hk = HGRN_KEY
    masks, signs = _scan_consts(rev)
    rows = [pl.ds(base + ci * SCAN_C, SCAN_C) for ci in range(nchunks)]
    local = [_scan_chunk_local(q_ref[r, :], k_ref[r, :], v_ref[r, :], b_ref[r, :], rev, masks, signs)
             for r in rows]
    st = [st_ref[0], st_ref[1]]
    for ci in (range(nchunks - 1, -1, -1) if rev else range(nchunks)):
        qe, scores, contrib, dec = local[ci]
        v = v_ref[rows[ci], :]
        w = _block_diag(st[0].astype(BF16), st[1].astype(BF16))
        o_ref[rows[ci], :] = (
            jnp.dot(scores, _block_diag(v[:, :hk], v[:, hk:]), preferred_element_type=F32)
            + lax.dot_general(qe, w, (((1,), (1,)), ((), ())), preferred_element_type=F32)
        ).astype(o_ref.dtype)
        st = [st[0] * dec[:, :hk] + contrib[0], st[1] * dec[:, hk:] + contrib[1]]
    st_ref[0] = st[0]
    st_ref[1] = st[1]


def _scan_kernel(nsub, sub, qf_ref, qb_ref, vf_ref, vb_ref, kf_ref, bf_ref, kb_ref, bb_ref,
                 of_ref, ob_ref, stf_ref, stb_ref):
    @pl.when(pl.program_id(2) == 0)
    def _():
        stf_ref[...] = jnp.zeros_like(stf_ref)
        stb_ref[...] = jnp.zeros_like(stb_ref)

    nchunks = sub // SCAN_C

    def body(i, carry):
        fbase = pl.multiple_of(i * sub, sub)
        bbase = pl.multiple_of((nsub - 1 - i) * sub, sub)
        _scan_direction(qf_ref, kf_ref, vf_ref, bf_ref, of_ref, stf_ref, False, fbase, nchunks)
        _scan_direction(qb_ref, kb_ref, vb_ref, bb_ref, ob_ref, stb_ref, True, bbase, nchunks)
        return carry

    lax.fori_loop(0, nsub, body, 0)


def _hgrn_scan(q, kf, bf, kb, bb, v, batch, seq, blk, sub):
    t = q.shape[0]
    nb = seq // blk
    fwd = pl.BlockSpec((blk, SCAN_PAIR), lambda b, h, n: (b * nb + n, h))
    bwd = pl.BlockSpec((blk, SCAN_PAIR), lambda b, h, n: (b * nb + nb - 1 - n, h))
    out = jax.ShapeDtypeStruct((t, D_MODEL), BF16)
    state = pltpu.VMEM((2, HGRN_KEY, HGRN_KEY), F32)
    return pl.pallas_call(
        functools.partial(_scan_kernel, blk // sub, sub),
        grid=(batch, D_MODEL // SCAN_PAIR, nb),
        in_specs=[fwd, bwd, fwd, bwd, fwd, fwd, bwd, bwd],
        out_specs=[fwd, bwd],
        out_shape=[out, out],
        scratch_shapes=[state, state],
        compiler_params=_cparams(("parallel", "parallel", "arbitrary")),
        name="hgrn_scan",
    )(q, q, v, v, kf, bf, kb, bb)


def _ffn_tail(x, p_ref, wi_ref, wo_ref, g_ref, b_ref, wg_ref, wp_ref, out_ref):
    xb = x.astype(BF16)
    acc = None
    for c in range(D_FF // FFN_FC):
        lo = c * FFN_FC
        gate = jnp.dot(xb, wi_ref[:, lo:lo + FFN_FC], preferred_element_type=F32)
        up = jnp.dot(xb, wi_ref[:, D_FF + lo:D_FF + lo + FFN_FC], preferred_element_type=F32)
        h = (gate * _sigmoid(gate) * up).astype(BF16)
        y = jnp.dot(h, wo_ref[lo:lo + FFN_FC, :], preferred_element_type=F32)
        acc = y if acc is None else acc + y
    x2 = _layer_norm(ALPHA * x + acc, g_ref[...], b_ref[...])
    egate = _sigmoid(jnp.dot(x2.astype(BF16), wg_ref[...], preferred_element_type=F32))
    proj = jnp.dot(p_ref[...].astype(BF16), wp_ref[...], preferred_element_type=F32)
    out_ref[...] = x2 + egate * proj


def _attn_tail_kernel(x_ref, o_ref, wm_ref, gm_ref, bm_ref, *rest):
    y = ALPHA * x_ref[...] + jnp.dot(o_ref[...], wm_ref[...], preferred_element_type=F32)
    _ffn_tail(_layer_norm(y, gm_ref[...], bm_ref[...]), *rest)


def _hgrn_tail_kernel(x_ref, of_ref, ob_ref, gt_ref, ng_ref, wm_ref, gm_ref, bm_ref, *rest):
    ys = []
    for h in range(HGRN_HEADS):
        sl = slice(h * HGRN_KEY, (h + 1) * HGRN_KEY)
        o = of_ref[:, sl].astype(F32) + ob_ref[:, sl].astype(F32)
        ms = jnp.mean(o * o, axis=-1, keepdims=True)
        ys.append((o * lax.rsqrt(ms + LN_EPS) * ng_ref[...] * gt_ref[:, sl].astype(F32)).astype(BF16))
    y = ALPHA * x_ref[...] + jnp.dot(jnp.concatenate(ys, axis=1), wm_ref[...], preferred_element_type=F32)
    _ffn_tail(_layer_norm(y, gm_ref[...], bm_ref[...]), *rest)


def _layer_tail(kernel_fn, name, x, mixer_inputs, mixer_specs, p, layer, w_in, w_out, g, b, w_gate, w_proj, tm):
    t = x.shape[0]
    row = pl.BlockSpec((tm, D_MODEL), lambda i: (i, 0))
    return pl.pallas_call(
        kernel_fn,
        grid=(t // tm,),
        in_specs=[row] + mixer_specs + [
            pl.BlockSpec((None, tm, PLE_DIM), lambda i: (layer, i, 0)),
            _resident((D_MODEL, 2 * D_FF)), _resident((D_FF, D_MODEL)),
            _resident((1, D_MODEL)), _resident((1, D_MODEL)),
            _resident((D_MODEL, D_MODEL)), _resident((PLE_DIM, D_MODEL))],
        out_specs=row,
        out_shape=jax.ShapeDtypeStruct((t, D_MODEL), F32),
        compiler_params=_cparams(("parallel",)),
        name=name,
    )(x, *mixer_inputs, p, w_in, w_out, g, b, w_gate, w_proj)


def _rope_tables(seq):
    half = ROPE_DIM // 2
    inv = ROPE_THETA ** (-jnp.arange(0, ROPE_DIM, 2, dtype=F32) / ROPE_DIM)
    ang = jnp.arange(seq, dtype=F32)[:, None] * inv[None, :]
    cos, sin = jnp.cos(ang), jnp.sin(ang)
    ones = jnp.ones((seq, HEAD_DIM - ROPE_DIM), F32)
    zeros = jnp.zeros((seq, half), F32)
    rest = jnp.zeros((seq, HEAD_DIM - ROPE_DIM), F32)
    reps = ROPE_TW // HEAD_DIM
    ctab = jnp.tile(jnp.concatenate([cos, cos, ones], axis=1), (1, reps))
    satab = jnp.tile(jnp.concatenate([zeros, sin, rest], axis=1), (1, reps))
    sbtab = jnp.tile(jnp.concatenate([-sin, zeros, rest], axis=1), (1, reps))
    return ctab, satab, sbtab


def _twice_per_head(w):
    d = w.shape[0]
    w = w.reshape(d, N_KV_HEADS, 1, HEAD_DIM)
    return jnp.broadcast_to(w, (d, N_KV_HEADS, 2, HEAD_DIM)).reshape(d, KV2_DIM)


def _tile(n, pref):
    while n % pref:
        pref //= 2
    return pref


def kernel(x, p, att_w_qkv, att_sink, att_w_o, hgrn_w_in, hgrn_lb_logits, hgrn_norm_g, hgrn_w_o,
           ln_mix_g, ln_mix_b, ffn_w_in, ffn_w_out, ln_ffn_g, ln_ffn_b, ple_w_gate, ple_w_proj):
    batch, seq, _ = x.shape
    t = batch * seq
    assert seq % ATT_TQ == 0
    tm = _tile(seq, ROW_TILE)
    blk = _tile(seq, SCAN_BLK)
    sub = _tile(blk, SCAN_SUB)
    assert tm % SCAN_C == 0 and sub % SCAN_C == 0
    row = pl.BlockSpec((tm, D_MODEL), lambda r: (r, 0))
    wres, vres = _resident((D_MODEL, D_MODEL)), _resident((1, D_MODEL))
    ctab, satab, sbtab = _rope_tables(seq)
    logits = hgrn_lb_logits.astype(F32).reshape(2 * DEPTH, D_MODEL)
    h = x.reshape(t, D_MODEL)
    pf = p.reshape(DEPTH, t, PLE_DIM)
    vec = lambda a: a.reshape(1, -1).astype(F32)
    for i in range(DEPTH):
        j = i // 2
        if i % 2 == 0:
            wq, wk, wv = (att_w_qkv[j][:, :Q_DIM], att_w_qkv[j][:, Q_DIM:Q_DIM + KV_DIM],
                          att_w_qkv[j][:, Q_DIM + KV_DIM:])
            w = jnp.concatenate([wq, _twice_per_head(wk), _twice_per_head(wv)], axis=1).astype(BF16)
            q, k2, v2 = _qkv_proj(h, w, ctab, satab, sbtab, seq, tm)
            o = _attention(q, k2, v2, att_sink[j].astype(F32) * LOG2E, batch, seq)
            tail = (_attn_tail_kernel, "attn_tail")
            mixer_inputs = [o, att_w_o[j].astype(BF16), vec(ln_mix_g[i]), vec(ln_mix_b[i])]
            mixer_specs = [row, wres, vres, vres]
        else:
            q, kf, bf, kb, bb, v, gate = _hgrn_in(h, hgrn_w_in[j].astype(BF16), logits, i, tm)
            of, ob = _hgrn_scan(q, kf, bf, kb, bb, v, batch, seq, blk, sub)
            tail = (_hgrn_tail_kernel, "hgrn_tail")
            mixer_inputs = [of, ob, gate, vec(hgrn_norm_g[j]), hgrn_w_o[j].astype(BF16),
                            vec(ln_mix_g[i]), vec(ln_mix_b[i])]
            mixer_specs = [row, row, row, _resident((1, HGRN_KEY)), wres, vres, vres]
        h = _layer_tail(*tail, h, mixer_inputs, mixer_specs, pf, i,
                        ffn_w_in[i].astype(BF16), ffn_w_out[i].astype(BF16),
                        vec(ln_ffn_g[i]), vec(ln_ffn_b[i]),
                        ple_w_gate[i].astype(BF16), ple_w_proj[i].astype(BF16), tm)
    return h.reshape(batch, seq, D_MODEL)
```

```python
import functools
import math

import jax
import jax.numpy as jnp
from jax import lax
from jax.experimental import pallas as pl
from jax.experimental.pallas import tpu as pltpu

F32 = jnp.float32
BF16 = jnp.bfloat16

D_MODEL = 1024
DEPTH = 4
HEAD_DIM = 64
N_Q_HEADS = 16
N_KV_HEADS = 4
GROUP = 4
Q_DIM = 1024
KV_DIM = 256
WINDOW = 128
ROPE_DIM = 16
ROPE_THETA = 500000.0
HGRN_HEADS = 8
HGRN_KEY = 128
D_FF = 2816
PLE_DIM = 256
ALPHA = (2 * DEPTH) ** 0.25
LN_EPS = 1e-5
LOG2E = math.log2(math.e)

LANES = 128
SUBLANES = 8
VMEM_LIMIT = 56 * 1024 * 1024

ROW_TILE = 512
ATT_TQ = 128
KV2_DIM = 2 * KV_DIM
ROPE_TW = 256
FFN_FC = 1408
SCAN_C = 64
SCAN_BASE = 16
SCAN_PAIR = 2 * HGRN_KEY
SCAN_SUB = 1024
SCAN_BLK = 1024
HGRN_IN_TN = 1024
HGRN_IN_TR = 512


def _cparams(sem):
    return pltpu.CompilerParams(dimension_semantics=sem, vmem_limit_bytes=VMEM_LIMIT)


def _resident(shape):
    return pl.BlockSpec(shape, lambda *_: (0,) * len(shape), pipeline_mode=pl.Buffered(1))


def _layer_norm(y, g, b):
    mu = jnp.mean(y, axis=-1, keepdims=True)
    d = y - mu
    var = jnp.mean(d * d, axis=-1, keepdims=True)
    return d * lax.rsqrt(var + LN_EPS) * g + b


def _sigmoid(z):
    return 0.5 * jnp.tanh(0.5 * z) + 0.5


def _qkv_kernel(x_ref, w_ref, c_ref, sa_ref, sb_ref, q_ref, k_ref, v_ref):
    xb = x_ref[...].astype(BF16)
    half = ROPE_DIM // 2

    def rope(a):
        return (a * c_ref[...] + pltpu.roll(a, half, 1) * sa_ref[...]
                + pltpu.roll(a, ROPE_TW - half, 1) * sb_ref[...])

    def proj(col):
        return jnp.dot(xb, w_ref[:, col:col + ROPE_TW], preferred_element_type=F32)

    for c in range(Q_DIM // ROPE_TW):
        q_ref[:, c * ROPE_TW:(c + 1) * ROPE_TW] = rope(proj(c * ROPE_TW) * (HEAD_DIM ** -0.5 * LOG2E)).astype(BF16)
    for c in range(KV2_DIM // ROPE_TW):
        k_ref[:, c * ROPE_TW:(c + 1) * ROPE_TW] = rope(proj(Q_DIM + c * ROPE_TW)).astype(BF16)
        v_ref[:, c * ROPE_TW:(c + 1) * ROPE_TW] = proj(Q_DIM + KV2_DIM + c * ROPE_TW).astype(BF16)


def _qkv_proj(x, w, ctab, satab, sbtab, seq, tm):
    t = x.shape[0]
    ns = seq // tm
    n_out = Q_DIM + 2 * KV2_DIM
    tab = pl.BlockSpec((tm, ROPE_TW), lambda i: (i % ns, 0))
    return pl.pallas_call(
        _qkv_kernel,
        grid=(t // tm,),
        in_specs=[pl.BlockSpec((tm, D_MODEL), lambda i: (i, 0)), _resident((D_MODEL, n_out)), tab, tab, tab],
        out_specs=[pl.BlockSpec((tm, Q_DIM), lambda i: (i, 0)),
                   pl.BlockSpec((tm, KV2_DIM), lambda i: (i, 0)),
                   pl.BlockSpec((tm, KV2_DIM), lambda i: (i, 0))],
        out_shape=[jax.ShapeDtypeStruct((t, Q_DIM), BF16),
                   jax.ShapeDtypeStruct((t, KV2_DIM), BF16),
                   jax.ShapeDtypeStruct((t, KV2_DIM), BF16)],
        compiler_params=_cparams(("parallel",)),
        name="attn_qkv",
    )(x, w, ctab, satab, sbtab)


def _attn_kernel(seq, sink_ref, q_ref, kp_ref, kc_ref, kn_ref, vp_ref, vc_ref, vn_ref, o_ref):
    i = pl.program_id(1)
    tq = ATT_TQ
    qpos = i * tq + lax.broadcasted_iota(jnp.int32, (tq, 3 * tq), 0)
    kpos = (i - 1) * tq + lax.broadcasted_iota(jnp.int32, (tq, 3 * tq), 1)
    valid = (jnp.abs(qpos - kpos) <= WINDOW) & (kpos >= 0) & (kpos < seq)
    bias = jnp.where(valid, 0.0, -jnp.inf).astype(F32)
    bias = jnp.concatenate([bias] * GROUP, axis=0)
    first = lax.broadcasted_iota(jnp.int32, (1, LANES), 1) < HEAD_DIM
    ones_a = jnp.broadcast_to(jnp.where(first, 1.0, 0.0).astype(BF16), (3 * tq, LANES))
    ones_b = jnp.broadcast_to(jnp.where(first, 0.0, 1.0).astype(BF16), (3 * tq, LANES))
    zero = jnp.zeros((), BF16)
    for hk in range(N_KV_HEADS):
        ks = slice(hk * LANES, (hk + 1) * LANES)
        k2 = jnp.concatenate([kp_ref[:, ks], kc_ref[:, ks], kn_ref[:, ks]], axis=0)
        v2 = jnp.concatenate([vp_ref[:, ks], vc_ref[:, ks], vn_ref[:, ks]], axis=0)
        va = jnp.concatenate([jnp.where(first, v2, zero), ones_a], axis=1)
        vb = jnp.concatenate([jnp.where(first, zero, v2), ones_b], axis=1)
        q0 = q_ref[:, (2 * hk) * LANES:(2 * hk + 1) * LANES]
        q1 = q_ref[:, (2 * hk + 1) * LANES:(2 * hk + 2) * LANES]
        lhs = jnp.concatenate([jnp.where(first, q0, zero), jnp.where(first, q1, zero),
                               jnp.where(first, zero, q0), jnp.where(first, zero, q1)], axis=0)
        heads = (4 * hk, 4 * hk + 2, 4 * hk + 1, 4 * hk + 3)
        sink = jnp.concatenate([jnp.full((tq, 1), sink_ref[h], F32) for h in heads], axis=0)
        s = lax.dot_general(lhs, k2, (((1,), (1,)), ((), ())), preferred_element_type=F32) + bias
        m = jnp.maximum(jnp.max(s, axis=-1, keepdims=True), sink)
        p = jnp.exp2(s - m).astype(BF16)
        es = jnp.exp2(sink - m)
        out = (jnp.dot(p[:2 * tq], va, preferred_element_type=F32)
               + jnp.dot(p[2 * tq:], vb, preferred_element_type=F32))
        den = out[:, LANES:] + jnp.where(first, es[:2 * tq], es[2 * tq:])
        o = (out[:, :LANES] / den).astype(BF16)
        o_ref[:, (2 * hk) * LANES:(2 * hk + 1) * LANES] = o[:tq]
        o_ref[:, (2 * hk + 1) * LANES:(2 * hk + 2) * LANES] = o[tq:]


def _attention(q, k2, v2, sink, batch, seq):
    t = q.shape[0]
    nq = seq // ATT_TQ

    def kv_spec(off):
        return pl.BlockSpec((ATT_TQ, KV2_DIM), lambda b, i: (b * nq + jnp.clip(i + off, 0, nq - 1), 0))

    return pl.pallas_call(
        functools.partial(_attn_kernel, seq),
        grid=(batch, nq),
        in_specs=[
            pl.BlockSpec(memory_space=pltpu.SMEM),
            pl.BlockSpec((ATT_TQ, Q_DIM), lambda b, i: (b * nq + i, 0)),
            kv_spec(-1), kv_spec(0), kv_spec(1), kv_spec(-1), kv_spec(0), kv_spec(1),
        ],
        out_specs=pl.BlockSpec((ATT_TQ, Q_DIM), lambda b, i: (b * nq + i, 0)),
        out_shape=jax.ShapeDtypeStruct((t, Q_DIM), BF16),
        compiler_params=_cparams(("parallel", "arbitrary")),
        name="attn_core",
    )(sink, q, k2, k2, k2, v2, v2, v2)


def _chunk_cumsum(g, rev):
    c, w = g.shape
    sub = lax.broadcasted_iota(jnp.int32, (SUBLANES, w), 0)
    groups = [g[i * SUBLANES:(i + 1) * SUBLANES] for i in range(c // SUBLANES)]
    sh = 1
    while sh < SUBLANES:
        if rev:
            groups = [x + jnp.where(sub < SUBLANES - sh, pltpu.roll(x, SUBLANES - sh, 0), 0.0) for x in groups]
        else:
            groups = [x + jnp.where(sub >= sh, pltpu.roll(x, sh, 0), 0.0) for x in groups]
        sh *= 2
    order = range(len(groups) - 1, -1, -1) if rev else range(len(groups))
    run = None
    out = [None] * len(groups)
    for i in order:
        gi = groups[i] if run is None else groups[i] + run
        out[i] = gi
        run = jnp.broadcast_to(gi[0:1] if rev else gi[SUBLANES - 1:SUBLANES], (SUBLANES, w))
    return jnp.concatenate(out, axis=0)


def _hgrn_in_kernel(layer, x_ref, w_ref, lg_ref, q_ref, kf_ref, bf_ref, kb_ref, bb_ref, v_ref, gt_ref):
    for r0 in range(0, x_ref.shape[0], HGRN_IN_TR):
        _hgrn_in_rows(layer, slice(r0, r0 + HGRN_IN_TR), x_ref, w_ref, lg_ref,
                      q_ref, kf_ref, bf_ref, kb_ref, bb_ref, v_ref, gt_ref)


def _hgrn_in_rows(layer, rws, x_ref, w_ref, lg_ref, q_ref, kf_ref, bf_ref, kb_ref, bb_ref, v_ref, gt_ref):
    xb = x_ref[rws, :].astype(BF16)
    tm = xb.shape[0]
    tn = HGRN_IN_TN

    def proj(sec, cols):
        lo = sec * D_MODEL + cols.start
        return jnp.dot(xb, w_ref[:, lo:lo + tn], preferred_element_type=F32)

    def silu(z):
        return z * _sigmoid(z)

    def lower_bound(d, cols):
        rows = [lg_ref[2 * dd + d:2 * dd + d + 1, cols] for dd in range(DEPTH)]
        m = functools.reduce(jnp.maximum, rows)
        e = [jnp.exp(r - m) for r in rows]
        return functools.reduce(jnp.add, e[1:layer + 1]) / functools.reduce(jnp.add, e)

    def forget(sec, d, cols, k_ref, b_ref):
        lb = lower_bound(d, cols)
        c = 0.5 * (1.0 - lb)
        t = c * jnp.tanh(0.5 * proj(sec, cols))
        k_ref[rws, cols] = (c - t).astype(BF16)
        g = jnp.log2((0.5 * (1.0 + lb)) + t)
        for ci in range(tm // SCAN_C):
            rows = slice(ci * SCAN_C, (ci + 1) * SCAN_C)
            b_ref[rws.start + rows.start:rws.start + rows.stop, cols] = _chunk_cumsum(g[rows], d == 1)

    for cc in range(D_MODEL // tn):
        cols = slice(cc * tn, (cc + 1) * tn)
        q_ref[rws, cols] = silu(proj(0, cols)).astype(BF16)
        forget(1, 0, cols, kf_ref, bf_ref)
        forget(2, 1, cols, kb_ref, bb_ref)
        v_ref[rws, cols] = proj(3, cols).astype(BF16)
        gt_ref[rws, cols] = silu(proj(4, cols)).astype(BF16)


def _hgrn_in(x, w, logits, layer, tm):
    t = x.shape[0]
    row = pl.BlockSpec((tm, D_MODEL), lambda i: (i, 0))
    bf = jax.ShapeDtypeStruct((t, D_MODEL), BF16)
    f32 = jax.ShapeDtypeStruct((t, D_MODEL), F32)
    return pl.pallas_call(
        functools.partial(_hgrn_in_kernel, layer),
        grid=(t // tm,),
        in_specs=[row, _resident((D_MODEL, 5 * D_MODEL)), _resident((2 * DEPTH, D_MODEL))],
        out_specs=[row] * 7,
        out_shape=[bf, bf, f32, bf, f32, bf, bf],
        compiler_params=_cparams(("parallel",)),
        name="hgrn_in",
    )(x, w, logits)


def _block_diag(a0, a1):
    z = jnp.zeros_like(a0)
    return jnp.concatenate([jnp.concatenate([a0, z], axis=1), jnp.concatenate([z, a1], axis=1)], axis=0)


def _scan_consts(rev):
    c = SCAN_C
    row = lax.broadcasted_iota(jnp.int32, (c, 2 * c), 0)
    col = lax.broadcasted_iota(jnp.int32, (c, 2 * c), 1) % c
    rowk = lax.broadcasted_iota(jnp.int32, (c, SCAN_PAIR), 0)
    masks, signs = {}, {}
    half = c // 2
    while half >= SCAN_BASE:
        size = 2 * half
        q_row = ((row % size) < half) if rev else ((row % size) >= half)
        k_col = ((col % size) >= half) if rev else ((col % size) < half)
        masks[size] = ((row // size) == (col // size)) & q_row & k_col
        q_rowk = ((rowk % size) < half) if rev else ((rowk % size) >= half)
        signs[size] = jnp.where(q_rowk, 1.0, -1.0).astype(F32)
        half //= 2
    masks[0] = ((row // SCAN_BASE) == (col // SCAN_BASE)) & ((col >= row) if rev else (col <= row))
    return masks, signs


def _scan_chunk_local(q, k, v, b, rev, masks, signs):
    c = SCAN_C
    hk = HGRN_KEY
    btot = b[0:1] if rev else b[c - 1:c]

    def ref_rows(size, offset):
        parts = [jnp.broadcast_to(b[p * size + offset:p * size + offset + 1], (size, SCAN_PAIR))
                 for p in range(c // size)]
        return parts[0] if len(parts) == 1 else jnp.concatenate(parts, axis=0)

    def pair_scores(qs, ks):
        kbd = _block_diag(ks[:, :hk], ks[:, hk:])
        return lax.dot_general(qs, kbd, (((1,), (1,)), ((), ())), preferred_element_type=F32)

    scores = None
    half = c // 2
    while half >= SCAN_BASE:
        size = 2 * half
        e = jnp.exp2((b - ref_rows(size, half if rev else half - 1)) * signs[size]).astype(BF16)
        s = jnp.where(masks[size], pair_scores(q * e, k * e), 0.0)
        scores = s if scores is None else scores + s
        half //= 2
    d = b - ref_rows(SCAN_BASE, SCAN_BASE // 2)
    s = pair_scores(q * jnp.exp2(d).astype(BF16), k * jnp.exp2(-d).astype(BF16))
    scores = scores + jnp.where(masks[0], s, 0.0)

    ke = k * jnp.exp2(btot - b).astype(BF16)
    contrib = [lax.dot_general(v[:, sl], ke[:, sl], (((0,), (0,)), ((), ())), preferred_element_type=F32)
               for sl in (slice(0, hk), slice(hk, 2 * hk))]
    return q * jnp.exp2(b).astype(BF16), scores.astype(BF16), contrib, jnp.exp2(btot)


def _scan_direction(q_ref, k_ref, v_ref, b_ref, o_ref, st_ref, rev, base, nchunks):
    hk = HGRN_KEY
    masks, signs = _scan_consts(rev)
    rows = [pl.ds(base + ci * SCAN_C, SCAN_C) for ci in range(nchunks)]
    local = [_scan_chunk_local(q_ref[r, :], k_ref[r, :], v_ref[r, :], b_ref[r, :], rev, masks, signs)
             for r in rows]
    st = [st_ref[0], st_ref[1]]
    for ci in (range(nchunks - 1, -1, -1) if rev else range(nchunks)):
        qe, scores, contrib, dec = local[ci]
        v = v_ref[rows[ci], :]
        w = _block_diag(st[0].astype(BF16), st[1].astype(BF16))
        o_ref[rows[ci], :] = (
            jnp.dot(scores, _block_diag(v[:, :hk], v[:, hk:]), preferred_element_type=F32)
            + lax.dot_general(qe, w, (((1,), (1,)), ((), ())), preferred_element_type=F32)
        ).astype(o_ref.dtype)
        st = [st[0] * dec[:, :hk] + contrib[0], st[1] * dec[:, hk:] + contrib[1]]
    st_ref[0] = st[0]
    st_ref[1] = st[1]


def _scan_kernel(nsub, sub, qf_ref, qb_ref, vf_ref, vb_ref, kf_ref, bf_ref, kb_ref, bb_ref,
                 of_ref, ob_ref, stf_ref, stb_ref):
    @pl.when(pl.program_id(2) == 0)
    def _():
        stf_ref[...] = jnp.zeros_like(stf_ref)
        stb_ref[...] = jnp.zeros_like(stb_ref)

    nchunks = sub // SCAN_C

    def body(i, carry):
        fbase = pl.multiple_of(i * sub, sub)
        bbase = pl.multiple_of((nsub - 1 - i) * sub, sub)
        _scan_direction(qf_ref, kf_ref, vf_ref, bf_ref, of_ref, stf_ref, False, fbase, nchunks)
        _scan_direction(qb_ref, kb_ref, vb_ref, bb_ref, ob_ref, stb_ref, True, bbase, nchunks)
        return carry

    lax.fori_loop(0, nsub, body, 0)


def _hgrn_scan(q, kf, bf, kb, bb, v, batch, seq, blk, sub):
    t = q.shape[0]
    nb = seq // blk
    fwd = pl.BlockSpec((blk, SCAN_PAIR), lambda b, h, n: (b * nb + n, h))
    bwd = pl.BlockSpec((blk, SCAN_PAIR), lambda b, h, n: (b * nb + nb - 1 - n, h))
    out = jax.ShapeDtypeStruct((t, D_MODEL), BF16)
    state = pltpu.VMEM((2, HGRN_KEY, HGRN_KEY), F32)
    return pl.pallas_call(
        functools.partial(_scan_kernel, blk // sub, sub),
        grid=(batch, D_MODEL // SCAN_PAIR, nb),
        in_specs=[fwd, bwd, fwd, bwd, fwd, fwd, bwd, bwd],
        out_specs=[fwd, bwd],
        out_shape=[out, out],
        scratch_shapes=[state, state],
        compiler_params=_cparams(("parallel", "parallel", "arbitrary")),
        name="hgrn_scan",
    )(q, q, v, v, kf, bf, kb, bb)


def _ffn_tail(x, p_ref, wi_ref, wo_ref, g_ref, b_ref, wg_ref, wp_ref, out_ref):
    xb = x.astype(BF16)
    acc = None
    for c in range(D_FF // FFN_FC):
        lo = c * FFN_FC
        gate = jnp.dot(xb, wi_ref[:, lo:lo + FFN_FC], preferred_element_type=F32)
        up = jnp.dot(xb, wi_ref[:, D_FF + lo:D_FF + lo + FFN_FC], preferred_element_type=F32)
        h = (gate * _sigmoid(gate) * up).astype(BF16)
        y = jnp.dot(h, wo_ref[lo:lo + FFN_FC, :], preferred_element_type=F32)
        acc = y if acc is None else acc + y
    x2 = _layer_norm(ALPHA * x + acc, g_ref[...], b_ref[...])
    egate = _sigmoid(jnp.dot(x2.astype(BF16), wg_ref[...], preferred_element_type=F32))
    proj = jnp.dot(p_ref[...].astype(BF16), wp_ref[...], preferred_element_type=F32)
    out_ref[...] = x2 + egate * proj


def _attn_tail_kernel(x_ref, o_ref, wm_ref, gm_ref, bm_ref, *rest):
    y = ALPHA * x_ref[...] + jnp.dot(o_ref[...], wm_ref[...], preferred_element_type=F32)
    _ffn_tail(_layer_norm(y, gm_ref[...], bm_ref[...]), *rest)


def _hgrn_tail_kernel(x_ref, of_ref, ob_ref, gt_ref, ng_ref, wm_ref, gm_ref, bm_ref, *rest):
    ys = []
    for h in range(HGRN_HEADS):
        sl = slice(h * HGRN_KEY, (h + 1) * HGRN_KEY)
        o = of_ref[:, sl].astype(F32) + ob_ref[:, sl].astype(F32)
        ms = jnp.mean(o * o, axis=-1, keepdims=True)
        ys.append((o * lax.rsqrt(ms + LN_EPS) * ng_ref[...] * gt_ref[:, sl].astype(F32)).astype(BF16))
    y = ALPHA * x_ref[...] + jnp.dot(jnp.concatenate(ys, axis=1), wm_ref[...], preferred_element_type=F32)
    _ffn_tail(_layer_norm(y, gm_ref[...], bm_ref[...]), *rest)


def _layer_tail(kernel_fn, name, x, mixer_inputs, mixer_specs, p, layer, w_in, w_out, g, b, w_gate, w_proj, tm):
    t = x.shape[0]
    row = pl.BlockSpec((tm, D_MODEL), lambda i: (i, 0))
    return pl.pallas_call(
        kernel_fn,
        grid=(t // tm,),
        in_specs=[row] + mixer_specs + [
            pl.BlockSpec((None, tm, PLE_DIM), lambda i: (layer, i, 0)),
            _resident((D_MODEL, 2 * D_FF)), _resident((D_FF, D_MODEL)),
            _resident((1, D_MODEL)), _resident((1, D_MODEL)),
            _resident((D_MODEL, D_MODEL)), _resident((PLE_DIM, D_MODEL))],
        out_specs=row,
        out_shape=jax.ShapeDtypeStruct((t, D_MODEL), F32),
        compiler_params=_cparams(("parallel",)),
        name=name,
    )(x, *mixer_inputs, p, w_in, w_out, g, b, w_gate, w_proj)


def _rope_tables(seq):
    half = ROPE_DIM // 2
    inv = ROPE_THETA ** (-jnp.arange(0, ROPE_DIM, 2, dtype=F32) / ROPE_DIM)
    ang = jnp.arange(seq, dtype=F32)[:, None] * inv[None, :]
    cos, sin = jnp.cos(ang), jnp.sin(ang)
    ones = jnp.ones((seq, HEAD_DIM - ROPE_DIM), F32)
    zeros = jnp.zeros((seq, half), F32)
    rest = jnp.zeros((seq, HEAD_DIM - ROPE_DIM), F32)
    reps = ROPE_TW // HEAD_DIM
    ctab = jnp.tile(jnp.concatenate([cos, cos, ones], axis=1), (1, reps))
    satab = jnp.tile(jnp.concatenate([zeros, sin, rest], axis=1), (1, reps))
    sbtab = jnp.tile(jnp.concatenate([-sin, zeros, rest], axis=1), (1, reps))
    return ctab, satab, sbtab


def _twice_per_head(w):
    d = w.shape[0]
    w = w.reshape(d, N_KV_HEADS, 1, HEAD_DIM)
    return jnp.broadcast_to(w, (d, N_KV_HEADS, 2, HEAD_DIM)).reshape(d, KV2_DIM)


def _tile(n, pref):
    while n % pref:
        pref //= 2
    return pref


def kernel(x, p, att_w_qkv, att_sink, att_w_o, hgrn_w_in, hgrn_lb_logits, hgrn_norm_g, hgrn_w_o,
           ln_mix_g, ln_mix_b, ffn_w_in, ffn_w_out, ln_ffn_g, ln_ffn_b, ple_w_gate, ple_w_proj):
    batch, seq, _ = x.shape
    t = batch * seq
    assert seq % ATT_TQ == 0
    tm = _tile(seq, ROW_TILE)
    blk = _tile(seq, SCAN_BLK)
    sub = _tile(blk, SCAN_SUB)
    assert tm % SCAN_C == 0 and sub % SCAN_C == 0
    row = pl.BlockSpec((tm, D_MODEL), lambda r: (r, 0))
    wres, vres = _resident((D_MODEL, D_MODEL)), _resident((1, D_MODEL))
    ctab, satab, sbtab = _rope_tables(seq)
    logits = hgrn_lb_logits.astype(F32).reshape(2 * DEPTH, D_MODEL)
    h = x.reshape(t, D_MODEL)
    pf = p.reshape(DEPTH, t, PLE_DIM)
    vec = lambda a: a.reshape(1, -1).astype(F32)
    for i in range(DEPTH):
        j = i // 2
        if i % 2 == 0:
            wq, wk, wv = (att_w_qkv[j][:, :Q_DIM], att_w_qkv[j][:, Q_DIM:Q_DIM + KV_DIM],
                          att_w_qkv[j][:, Q_DIM + KV_DIM:])
            w = jnp.concatenate([wq, _twice_per_head(wk), _twice_per_head(wv)], axis=1).astype(BF16)
            q, k2, v2 = _qkv_proj(h, w, ctab, satab, sbtab, seq, tm)
            o = _attention(q, k2, v2, att_sink[j].astype(F32) * LOG2E, batch, seq)
            tail = (_attn_tail_kernel, "attn_tail")
            mixer_inputs = [o, att_w_o[j].astype(BF16), vec(ln_mix_g[i]), vec(ln_mix_b[i])]
            mixer_specs = [row, wres, vres, vres]
        else:
            q, kf, bf, kb, bb, v, gate = _hgrn_in(h, hgrn_w_in[j].astype(BF16), logits, i, tm)
            of, ob = _hgrn_scan(q, kf, bf, kb, bb, v, batch, seq, blk, sub)
            tail = (_hgrn_tail_kernel, "hgrn_tail")
            mixer_inputs = [of, ob, gate, vec(hgrn_norm_g[j]), hgrn_w_o[j].astype(BF16),
                            vec(ln_mix_g[i]), vec(ln_mix_b[i])]
            mixer_specs = [row, row, row, _resident((1, HGRN_KEY)), wres, vres, vres]
        h = _layer_tail(*tail, h, mixer_inputs, mixer_specs, pf, i,
                        ffn_w_in[i].astype(BF16), ffn_w_out[i].astype(BF16),
                        vec(ln_ffn_g[i]), vec(ln_ffn_b[i]),
                        ple_w_gate[i].astype(BF16), ple_w_proj[i].astype(BF16), tm)
    return h.reshape(batch, seq, D_MODEL)
```

```python
import functools
import math

import jax
import jax.numpy as jnp
from jax import lax
from jax.experimental import pallas as pl
from jax.experimental.pallas import tpu as pltpu

F32 = jnp.float32
BF16 = jnp.bfloat16

D_MODEL = 1024
DEPTH = 4
HEAD_DIM = 64
N_Q_HEADS = 16
N_KV_HEADS = 4
GROUP = 4
Q_DIM = 1024
KV_DIM = 256
WINDOW = 128
ROPE_DIM = 16
ROPE_THETA = 500000.0
HGRN_HEADS = 8
HGRN_KEY = 128
D_FF = 2816
PLE_DIM = 256
ALPHA = (2 * DEPTH) ** 0.25
LN_EPS = 1e-5
LOG2E = math.log2(math.e)

LANES = 128
SUBLANES = 8
VMEM_LIMIT = 56 * 1024 * 1024

ROW_TILE = 512
ATT_TQ = 128
KV2_DIM = 2 * KV_DIM
ROPE_TW = 256
FFN_FC = 1408
SCAN_C = 64
SCAN_BASE = 16
SCAN_PAIR = 2 * HGRN_KEY
SCAN_SUB = 1024
SCAN_BLK = 1024
HGRN_IN_TN = 1024
HGRN_IN_TR = 512


def _cparams(sem):
    return pltpu.CompilerParams(dimension_semantics=sem, vmem_limit_bytes=VMEM_LIMIT)


def _resident(shape):
    return pl.BlockSpec(shape, lambda *_: (0,) * len(shape), pipeline_mode=pl.Buffered(1))


def _layer_norm(y, g, b):
    mu = jnp.mean(y, axis=-1, keepdims=True)
    d = y - mu
    var = jnp.mean(d * d, axis=-1, keepdims=True)
    return d * lax.rsqrt(var + LN_EPS) * g + b


def _sigmoid(z):
    return 0.5 * jnp.tanh(0.5 * z) + 0.5


def _qkv_kernel(x_ref, w_ref, c_ref, sa_ref, sb_ref, q_ref, k_ref, v_ref):
    xb = x_ref[...].astype(BF16)
    half = ROPE_DIM // 2

    def rope(a):
        return (a * c_ref[...] + pltpu.roll(a, half, 1) * sa_ref[...]
                + pltpu.roll(a, ROPE_TW - half, 1) * sb_ref[...])

    def proj(col):
        return jnp.dot(xb, w_ref[:, col:col + ROPE_TW], preferred_element_type=F32)

    for c in range(Q_DIM // ROPE_TW):
        q_ref[:, c * ROPE_TW:(c + 1) * ROPE_TW] = rope(proj(c * ROPE_TW) * (HEAD_DIM ** -0.5 * LOG2E)).astype(BF16)
    for c in range(KV2_DIM // ROPE_TW):
        k_ref[:, c * ROPE_TW:(c + 1) * ROPE_TW] = rope(proj(Q_DIM + c * ROPE_TW)).astype(BF16)
        v_ref[:, c * ROPE_TW:(c + 1) * ROPE_TW] = proj(Q_DIM + KV2_DIM + c * ROPE_TW).astype(BF16)


def _qkv_proj(x, w, ctab, satab, sbtab, seq, tm):
    t = x.shape[0]
    ns = seq // tm
    n_out = Q_DIM + 2 * KV2_DIM
    tab = pl.BlockSpec((tm, ROPE_TW), lambda i: (i % ns, 0))
    return pl.pallas_call(
        _qkv_kernel,
        grid=(t // tm,),
        in_specs=[pl.BlockSpec((tm, D_MODEL), lambda i: (i, 0)), _resident((D_MODEL, n_out)), tab, tab, tab],
        out_specs=[pl.BlockSpec((tm, Q_DIM), lambda i: (i, 0)),
                   pl.BlockSpec((tm, KV2_DIM), lambda i: (i, 0)),
                   pl.BlockSpec((tm, KV2_DIM), lambda i: (i, 0))],
        out_shape=[jax.ShapeDtypeStruct((t, Q_DIM), BF16),
                   jax.ShapeDtypeStruct((t, KV2_DIM), BF16),
                   jax.ShapeDtypeStruct((t, KV2_DIM), BF16)],
        compiler_params=_cparams(("parallel",)),
        name="attn_qkv",
    )(x, w, ctab, satab, sbtab)


def _attn_kernel(seq, sink_ref, q_ref, kp_ref, kc_ref, kn_ref, vp_ref, vc_ref, vn_ref, o_ref):
    i = pl.program_id(1)
    tq = ATT_TQ
    qpos = i * tq + lax.broadcasted_iota(jnp.int32, (tq, 3 * tq), 0)
    kpos = (i - 1) * tq + lax.broadcasted_iota(jnp.int32, (tq, 3 * tq), 1)
    valid = (jnp.abs(qpos - kpos) <= WINDOW) & (kpos >= 0) & (kpos < seq)
    bias = jnp.where(valid, 0.0, -jnp.inf).astype(F32)
    bias = jnp.concatenate([bias] * GROUP, axis=0)
    first = lax.broadcasted_iota(jnp.int32, (1, LANES), 1) < HEAD_DIM
    ones_a = jnp.broadcast_to(jnp.where(first, 1.0, 0.0).astype(BF16), (3 * tq, LANES))
    ones_b = jnp.broadcast_to(jnp.where(first, 0.0, 1.0).astype(BF16), (3 * tq, LANES))
    zero = jnp.zeros((), BF16)

    def logits(hk):
        ks = slice(hk * LANES, (hk + 1) * LANES)
        k2 = jnp.concatenate([kp_ref[:, ks], kc_ref[:, ks], kn_ref[:, ks]], axis=0)
        q0 = q_ref[:, (2 * hk) * LANES:(2 * hk + 1) * LANES]
        q1 = q_ref[:, (2 * hk + 1) * LANES:(2 * hk + 2) * LANES]
        lhs = jnp.concatenate([jnp.where(first, q0, zero), jnp.where(first, q1, zero),
                               jnp.where(first, zero, q0), jnp.where(first, zero, q1)], axis=0)
        return lax.dot_general(lhs, k2, (((1,), (1,)), ((), ())), preferred_element_type=F32) + bias

    def weights(hk, s):
        heads = (4 * hk, 4 * hk + 2, 4 * hk + 1, 4 * hk + 3)
        sink = jnp.concatenate([jnp.full((tq, LANES), sink_ref[h], F32) for h in heads], axis=0)
        m = jnp.maximum(jnp.broadcast_to(jnp.max(s, axis=-1, keepdims=True), sink.shape), sink)
        p = jnp.concatenate([jnp.exp2(s[:, j * LANES:(j + 1) * LANES] - m).astype(BF16) for j in range(3)],
                            axis=1)
        return p, jnp.exp2(sink - m)

    def outputs(hk, p, es):
        ks = slice(hk * LANES, (hk + 1) * LANES)
        v2 = jnp.concatenate([vp_ref[:, ks], vc_ref[:, ks], vn_ref[:, ks]], axis=0)
        va = jnp.concatenate([jnp.where(first, v2, zero), ones_a], axis=1)
        vb = jnp.concatenate([jnp.where(first, zero, v2), ones_b], axis=1)
        out = (jnp.dot(p[:2 * tq], va, preferred_element_type=F32)
               + jnp.dot(p[2 * tq:], vb, preferred_element_type=F32))
        den = out[:, LANES:] + jnp.where(first, es[:2 * tq], es[2 * tq:])
        o = (out[:, :LANES] / den).astype(BF16)
        o_ref[:, (2 * hk) * LANES:(2 * hk + 1) * LANES] = o[:tq]
        o_ref[:, (2 * hk + 1) * LANES:(2 * hk + 2) * LANES] = o[tq:]

    scores = [logits(hk) for hk in range(N_KV_HEADS)]
    probs = [weights(hk, scores[hk]) for hk in range(N_KV_HEADS)]
    for hk in range(N_KV_HEADS):
        outputs(hk, *probs[hk])


def _attention(q, k2, v2, sink, batch, seq):
    t = q.shape[0]
    nq = seq // ATT_TQ

    def kv_spec(off):
        return pl.BlockSpec((ATT_TQ, KV2_DIM), lambda b, i: (b * nq + jnp.clip(i + off, 0, nq - 1), 0))

    return pl.pallas_call(
        functools.partial(_attn_kernel, seq),
        grid=(batch, nq),
        in_specs=[
            pl.BlockSpec(memory_space=pltpu.SMEM),
            pl.BlockSpec((ATT_TQ, Q_DIM), lambda b, i: (b * nq + i, 0)),
            kv_spec(-1), kv_spec(0), kv_spec(1), kv_spec(-1), kv_spec(0), kv_spec(1),
        ],
        out_specs=pl.BlockSpec((ATT_TQ, Q_DIM), lambda b, i: (b * nq + i, 0)),
        out_shape=jax.ShapeDtypeStruct((t, Q_DIM), BF16),
        compiler_params=_cparams(("parallel", "arbitrary")),
        name="attn_core",
    )(sink, q, k2, k2, k2, v2, v2, v2)


def _chunk_cumsum(g, rev):
    c, w = g.shape
    sub = lax.broadcasted_iota(jnp.int32, (SUBLANES, w), 0)
    groups = [g[i * SUBLANES:(i + 1) * SUBLANES] for i in range(c // SUBLANES)]
    sh = 1
    while sh < SUBLANES:
        if rev:
            groups = [x + jnp.where(sub < SUBLANES - sh, pltpu.roll(x, SUBLANES - sh, 0), 0.0) for x in groups]
        else:
            groups = [x + jnp.where(sub >= sh, pltpu.roll(x, sh, 0), 0.0) for x in groups]
        sh *= 2
    order = range(len(groups) - 1, -1, -1) if rev else range(len(groups))
    run = None
    out = [None] * len(groups)
    for i in order:
        gi = groups[i] if run is None else groups[i] + run
        out[i] = gi
        run = jnp.broadcast_to(gi[0:1] if rev else gi[SUBLANES - 1:SUBLANES], (SUBLANES, w))
    return jnp.concatenate(out, axis=0)


def _hgrn_in_kernel(layer, x_ref, w_ref, lg_ref, q_ref, kf_ref, bf_ref, kb_ref, bb_ref, v_ref, gt_ref):
    for r0 in range(0, x_ref.shape[0], HGRN_IN_TR):
        _hgrn_in_rows(layer, slice(r0, r0 + HGRN_IN_TR), x_ref, w_ref, lg_ref,
                      q_ref, kf_ref, bf_ref, kb_ref, bb_ref, v_ref, gt_ref)


def _hgrn_in_rows(layer, rws, x_ref, w_ref, lg_ref, q_ref, kf_ref, bf_ref, kb_ref, bb_ref, v_ref, gt_ref):
    xb = x_ref[rws, :].astype(BF16)
    tm = xb.shape[0]
    tn = HGRN_IN_TN

    def proj(sec, cols):
        lo = sec * D_MODEL + cols.start
        return jnp.dot(xb, w_ref[:, lo:lo + tn], preferred_element_type=F32)

    def silu(z):
        return z * _sigmoid(z)

    def lower_bound(d, cols):
        rows = [lg_ref[2 * dd + d:2 * dd + d + 1, cols] for dd in range(DEPTH)]
        m = functools.reduce(jnp.maximum, rows)
        e = [jnp.exp(r - m) for r in rows]
        return functools.reduce(jnp.add, e[1:layer + 1]) / functools.reduce(jnp.add, e)

    def forget(sec, d, cols, k_ref, b_ref):
        lb = lower_bound(d, cols)
        c = 0.5 * (1.0 - lb)
        t = c * jnp.tanh(0.5 * proj(sec, cols))
        k_ref[rws, cols] = (c - t).astype(BF16)
        g = jnp.log2((0.5 * (1.0 + lb)) + t)
        for ci in range(tm // SCAN_C):
            rows = slice(ci * SCAN_C, (ci + 1) * SCAN_C)
            b_ref[rws.start + rows.start:rws.start + rows.stop, cols] = _chunk_cumsum(g[rows], d == 1)

    for cc in range(D_MODEL // tn):
        cols = slice(cc * tn, (cc + 1) * tn)
        q_ref[rws, cols] = silu(proj(0, cols)).astype(BF16)
        forget(1, 0, cols, kf_ref, bf_ref)
        forget(2, 1, cols, kb_ref, bb_ref)
        v_ref[rws, cols] = proj(3, cols).astype(BF16)
        gt_ref[rws, cols] = silu(proj(4, cols)).astype(BF16)


def _hgrn_in(x, w, logits, layer, tm):
    t = x.shape[0]
    row = pl.BlockSpec((tm, D_MODEL), lambda i: (i, 0))
    bf = jax.ShapeDtypeStruct((t, D_MODEL), BF16)
    f32 = jax.ShapeDtypeStruct((t, D_MODEL), F32)
    return pl.pallas_call(
        functools.partial(_hgrn_in_kernel, layer),
        grid=(t // tm,),
        in_specs=[row, _resident((D_MODEL, 5 * D_MODEL)), _resident((2 * DEPTH, D_MODEL))],
        out_specs=[row] * 7,
        out_shape=[bf, bf, f32, bf, f32, bf, bf],
        compiler_params=_cparams(("parallel",)),
        name="hgrn_in",
    )(x, w, logits)


def _block_diag(a0, a1):
    z = jnp.zeros_like(a0)
    return jnp.concatenate([jnp.concatenate([a0, z], axis=1), jnp.concatenate([z, a1], axis=1)], axis=0)


def _scan_consts(rev):
    c = SCAN_C
    row = lax.broadcasted_iota(jnp.int32, (c, 2 * c), 0)
    col = lax.broadcasted_iota(jnp.int32, (c, 2 * c), 1) % c
    rowk = lax.broadcasted_iota(jnp.int32, (c, SCAN_PAIR), 0)
    masks, signs = {}, {}
    half = c // 2
    while half >= SCAN_BASE:
        size = 2 * half
        q_row = ((row % size) < half) if rev else ((row % size) >= half)
        k_col = ((col % size) >= half) if rev else ((col % size) < half)
        masks[size] = ((row // size) == (col // size)) & q_row & k_col
        q_rowk = ((rowk % size) < half) if rev else ((rowk % size) >= half)
        signs[size] = jnp.where(q_rowk, 1.0, -1.0).astype(F32)
        half //= 2
    masks[0] = ((row // SCAN_BASE) == (col // SCAN_BASE)) & ((col >= row) if rev else (col <= row))
    return masks, signs


def _scan_chunk_local(q, k, v, b, rev, masks, signs):
    c = SCAN_C
    hk = HGRN_KEY
    btot = b[0:1] if rev else b[c - 1:c]

    def ref_rows(size, offset):
        parts = [jnp.broadcast_to(b[p * size + offset:p * size + offset + 1], (size, SCAN_PAIR))
                 for p in range(c // size)]
        return parts[0] if len(parts) == 1 else jnp.concatenate(parts, axis=0)

    def pair_scores(qs, ks):
        kbd = _block_diag(ks[:, :hk], ks[:, hk:])
        return lax.dot_general(qs, kbd, (((1,), (1,)), ((), ())), preferred_element_type=F32)

    scores = None
    half = c // 2
    while half >= SCAN_BASE:
        size = 2 * half
        e = jnp.exp2((b - ref_rows(size, half if rev else half - 1)) * signs[size]).astype(BF16)
        s = jnp.where(masks[size], pair_scores(q * e, k * e), 0.0)
        scores = s if scores is None else scores + s
        half //= 2
    d = b - ref_rows(SCAN_BASE, SCAN_BASE // 2)
    s = pair_scores(q * jnp.exp2(d).astype(BF16), k * jnp.exp2(-d).astype(BF16))
    scores = scores + jnp.where(masks[0], s, 0.0)

    ke = k * jnp.exp2(btot - b).astype(BF16)
    contrib = [lax.dot_general(v[:, sl], ke[:, sl], (((0,), (0,)), ((), ())), preferred_element_type=F32)
               for sl in (slice(0, hk), slice(hk, 2 * hk))]
    return q * jnp.exp2(b).astype(BF16), scores.astype(BF16), contrib, jnp.exp2(btot)


def _scan_local(q_ref, k_ref, v_ref, b_ref, rev, base, nchunks):
    masks, signs = _scan_consts(rev)
    rows = [pl.ds(base + ci * SCAN_C, SCAN_C) for ci in range(nchunks)]
    return rows, [_scan_chunk_local(q_ref[r, :], k_ref[r, :], v_ref[r, :], b_ref[r, :], rev, masks, signs)
                  for r in rows]


def _scan_serial(rows, local, v_ref, o_ref, st_ref, rev):
    hk = HGRN_KEY
    nchunks = len(rows)
    st = [st_ref[0], st_ref[1]]
    for ci in (range(nchunks - 1, -1, -1) if rev else range(nchunks)):
        qe, scores, contrib, dec = local[ci]
        v = v_ref[rows[ci], :]
        w = _block_diag(st[0].astype(BF16), st[1].astype(BF16))
        o_ref[rows[ci], :] = (
            jnp.dot(scores, _block_diag(v[:, :hk], v[:, hk:]), preferred_element_type=F32)
            + lax.dot_general(qe, w, (((1,), (1,)), ((), ())), preferred_element_type=F32)
        ).astype(o_ref.dtype)
        st = [st[0] * dec[:, :hk] + contrib[0], st[1] * dec[:, hk:] + contrib[1]]
    st_ref[0] = st[0]
    st_ref[1] = st[1]


def _scan_kernel(nsub, sub, qf_ref, qb_ref, vf_ref, vb_ref, kf_ref, bf_ref, kb_ref, bb_ref,
                 of_ref, ob_ref, stf_ref, stb_ref):
    @pl.when(pl.program_id(2) == 0)
    def _():
        stf_ref[...] = jnp.zeros_like(stf_ref)
        stb_ref[...] = jnp.zeros_like(stb_ref)

    nchunks = sub // SCAN_C

    def body(i, carry):
        fbase = pl.multiple_of(i * sub, sub)
        bbase = pl.multiple_of((nsub - 1 - i) * sub, sub)
        rows_f, local_f = _scan_local(qf_ref, kf_ref, vf_ref, bf_ref, False, fbase, nchunks)
        rows_b, local_b = _scan_local(qb_ref, kb_ref, vb_ref, bb_ref, True, bbase, nchunks)
        _scan_serial(rows_f, local_f, vf_ref, of_ref, stf_ref, False)
        _scan_serial(rows_b, local_b, vb_ref, ob_ref, stb_ref, True)
        return carry

    lax.fori_loop(0, nsub, body, 0)


def _hgrn_scan(q, kf, bf, kb, bb, v, batch, seq, blk, sub):
    t = q.shape[0]
    nb = seq // blk
    fwd = pl.BlockSpec((blk, SCAN_PAIR), lambda b, h, n: (b * nb + n, h))
    bwd = pl.BlockSpec((blk, SCAN_PAIR), lambda b, h, n: (b * nb + nb - 1 - n, h))
    out = jax.ShapeDtypeStruct((t, D_MODEL), BF16)
    state = pltpu.VMEM((2, HGRN_KEY, HGRN_KEY), F32)
    return pl.pallas_call(
        functools.partial(_scan_kernel, blk // sub, sub),
        grid=(batch, D_MODEL // SCAN_PAIR, nb),
        in_specs=[fwd, bwd, fwd, bwd, fwd, fwd, bwd, bwd],
        out_specs=[fwd, bwd],
        out_shape=[out, out],
        scratch_shapes=[state, state],
        compiler_params=_cparams(("parallel", "parallel", "arbitrary")),
        name="hgrn_scan",
    )(q, q, v, v, kf, bf, kb, bb)


def _ffn_tail(x, p_ref, wi_ref, wo_ref, g_ref, b_ref, wg_ref, wp_ref, out_ref):
    xb = x.astype(BF16)
    acc = None
    for c in range(D_FF // FFN_FC):
        lo = c * FFN_FC
        gate = jnp.dot(xb, wi_ref[:, lo:lo + FFN_FC], preferred_element_type=F32)
        up = jnp.dot(xb, wi_ref[:, D_FF + lo:D_FF + lo + FFN_FC], preferred_element_type=F32)
        h = (gate * _sigmoid(gate) * up).astype(BF16)
        y = jnp.dot(h, wo_ref[lo:lo + FFN_FC, :], preferred_element_type=F32)
        acc = y if acc is None else acc + y
    x2 = _layer_norm(ALPHA * x + acc, g_ref[...], b_ref[...])
    egate = _sigmoid(jnp.dot(x2.astype(BF16), wg_ref[...], preferred_element_type=F32))
    proj = jnp.dot(p_ref[...].astype(BF16), wp_ref[...], preferred_element_type=F32)
    out_ref[...] = x2 + egate * proj


def _attn_tail_kernel(x_ref, o_ref, wm_ref, gm_ref, bm_ref, *rest):
    y = ALPHA * x_ref[...] + jnp.dot(o_ref[...], wm_ref[...], preferred_element_type=F32)
    _ffn_tail(_layer_norm(y, gm_ref[...], bm_ref[...]), *rest)


def _hgrn_tail_kernel(x_ref, of_ref, ob_ref, gt_ref, ng_ref, wm_ref, gm_ref, bm_ref, *rest):
    ys = []
    for h in range(HGRN_HEADS):
        sl = slice(h * HGRN_KEY, (h + 1) * HGRN_KEY)
        o = of_ref[:, sl].astype(F32) + ob_ref[:, sl].astype(F32)
        ms = jnp.mean(o * o, axis=-1, keepdims=True)
        ys.append((o * lax.rsqrt(ms + LN_EPS) * ng_ref[...] * gt_ref[:, sl].astype(F32)).astype(BF16))
    y = ALPHA * x_ref[...] + jnp.dot(jnp.concatenate(ys, axis=1), wm_ref[...], preferred_element_type=F32)
    _ffn_tail(_layer_norm(y, gm_ref[...], bm_ref[...]), *rest)


def _layer_tail(kernel_fn, name, x, mixer_inputs, mixer_specs, p, layer, w_in, w_out, g, b, w_gate, w_proj, tm):
    t = x.shape[0]
    row = pl.BlockSpec((tm, D_MODEL), lambda i: (i, 0))
    return pl.pallas_call(
        kernel_fn,
        grid=(t // tm,),
        in_specs=[row] + mixer_specs + [
            pl.BlockSpec((None, tm, PLE_DIM), lambda i: (layer, i, 0)),
            _resident((D_MODEL, 2 * D_FF)), _resident((D_FF, D_MODEL)),
            _resident((1, D_MODEL)), _resident((1, D_MODEL)),
            _resident((D_MODEL, D_MODEL)), _resident((PLE_DIM, D_MODEL))],
        out_specs=row,
        out_shape=jax.ShapeDtypeStruct((t, D_MODEL), F32),
        compiler_params=_cparams(("parallel",)),
        name=name,
    )(x, *mixer_inputs, p, w_in, w_out, g, b, w_gate, w_proj)


def _rope_tables(seq):
    half = ROPE_DIM // 2
    inv = ROPE_THETA ** (-jnp.arange(0, ROPE_DIM, 2, dtype=F32) / ROPE_DIM)
    ang = jnp.arange(seq, dtype=F32)[:, None] * inv[None, :]
    cos, sin = jnp.cos(ang), jnp.sin(ang)
    ones = jnp.ones((seq, HEAD_DIM - ROPE_DIM), F32)
    zeros = jnp.zeros((seq, half), F32)
    rest = jnp.zeros((seq, HEAD_DIM - ROPE_DIM), F32)
    reps = ROPE_TW // HEAD_DIM
    ctab = jnp.tile(jnp.concatenate([cos, cos, ones], axis=1), (1, reps))
    satab = jnp.tile(jnp.concatenate([zeros, sin, rest], axis=1), (1, reps))
    sbtab = jnp.tile(jnp.concatenate([-sin, zeros, rest], axis=1), (1, reps))
    return ctab, satab, sbtab


def _twice_per_head(w):
    d = w.shape[0]
    w = w.reshape(d, N_KV_HEADS, 1, HEAD_DIM)
    return jnp.broadcast_to(w, (d, N_KV_HEADS, 2, HEAD_DIM)).reshape(d, KV2_DIM)


def _tile(n, pref):
    while n % pref:
        pref //= 2
    return pref


def kernel(x, p, att_w_qkv, att_sink, att_w_o, hgrn_w_in, hgrn_lb_logits, hgrn_norm_g, hgrn_w_o,
           ln_mix_g, ln_mix_b, ffn_w_in, ffn_w_out, ln_ffn_g, ln_ffn_b, ple_w_gate, ple_w_proj):
    batch, seq, _ = x.shape
    t = batch * seq
    assert seq % ATT_TQ == 0
    tm = _tile(seq, ROW_TILE)
    blk = _tile(seq, SCAN_BLK)
    sub = _tile(blk, SCAN_SUB)
    assert tm % SCAN_C == 0 and sub % SCAN_C == 0
    row = pl.BlockSpec((tm, D_MODEL), lambda r: (r, 0))
    wres, vres = _resident((D_MODEL, D_MODEL)), _resident((1, D_MODEL))
    ctab, satab, sbtab = _rope_tables(seq)
    logits = hgrn_lb_logits.astype(F32).reshape(2 * DEPTH, D_MODEL)
    h = x.reshape(t, D_MODEL)
    pf = p.reshape(DEPTH, t, PLE_DIM)
    vec = lambda a: a.reshape(1, -1).astype(F32)
    for i in range(DEPTH):
        j = i // 2
        if i % 2 == 0:
            wq, wk, wv = (att_w_qkv[j][:, :Q_DIM], att_w_qkv[j][:, Q_DIM:Q_DIM + KV_DIM],
                          att_w_qkv[j][:, Q_DIM + KV_DIM:])
            w = jnp.concatenate([wq, _twice_per_head(wk), _twice_per_head(wv)], axis=1).astype(BF16)
            q, k2, v2 = _qkv_proj(h, w, ctab, satab, sbtab, seq, tm)
            o = _attention(q, k2, v2, att_sink[j].astype(F32) * LOG2E, batch, seq)
            tail = (_attn_tail_kernel, "attn_tail")
            mixer_inputs = [o, att_w_o[j].astype(BF16), vec(ln_mix_g[i]), vec(ln_mix_b[i])]
            mixer_specs = [row, wres, vres, vres]
        else:
            q, kf, bf, kb, bb, v, gate = _hgrn_in(h, hgrn_w_in[j].astype(BF16), logits, i, tm)
            of, ob = _hgrn_scan(q, kf, bf, kb, bb, v, batch, seq, blk, sub)
            tail = (_hgrn_tail_kernel, "hgrn_tail")
            mixer_inputs = [of, ob, gate, vec(hgrn_norm_g[j]), hgrn_w_o[j].astype(BF16),
                            vec(ln_mix_g[i]), vec(ln_mix_b[i])]
            mixer_specs = [row, row, row, _resident((1, HGRN_KEY)), wres, vres, vres]
        h = _layer_tail(*tail, h, mixer_inputs, mixer_specs, pf, i,
                        ffn_w_in[i].astype(BF16), ffn_w_out[i].astype(BF16),
                        vec(ln_ffn_g[i]), vec(ln_ffn_b[i]),
                        ple_w_gate[i].astype(BF16), ple_w_proj[i].astype(BF16), tm)
    return h.reshape(batch, seq, D_MODEL)
```

```python
import functools
import math

import jax
import jax.numpy as jnp
from jax import lax
from jax.experimental import pallas as pl
from jax.experimental.pallas import tpu as pltpu

F32 = jnp.float32
BF16 = jnp.bfloat16

D_MODEL = 1024
DEPTH = 4
HEAD_DIM = 64
N_Q_HEADS = 16
N_KV_HEADS = 4
GROUP = 4
Q_DIM = 1024
KV_DIM = 256
WINDOW = 128
ROPE_DIM = 16
ROPE_THETA = 500000.0
HGRN_HEADS = 8
HGRN_KEY = 128
D_FF = 2816
PLE_DIM = 256
ALPHA = (2 * DEPTH) ** 0.25
LN_EPS = 1e-5
LOG2E = math.log2(math.e)

LANES = 128
SUBLANES = 8
VMEM_LIMIT = 56 * 1024 * 1024

ROW_TILE = 512
ATT_TQ = 128
KV2_DIM = 2 * KV_DIM
ROPE_TW = 256
FFN_FC = 1408
SCAN_C = 64
SCAN_BASE = 16
SCAN_PAIR = 2 * HGRN_KEY
SCAN_SUB = 1024
SCAN_BLK = 1024
HGRN_IN_TN = 1024
HGRN_IN_TR = 512


def _cparams(sem):
    return pltpu.CompilerParams(dimension_semantics=sem, vmem_limit_bytes=VMEM_LIMIT)


def _resident(shape):
    return pl.BlockSpec(shape, lambda *_: (0,) * len(shape), pipeline_mode=pl.Buffered(1))


def _layer_norm(y, g, b):
    mu = jnp.mean(y, axis=-1, keepdims=True)
    d = y - mu
    var = jnp.mean(d * d, axis=-1, keepdims=True)
    return d * lax.rsqrt(var + LN_EPS) * g + b


def _sigmoid(z):
    return 0.5 * jnp.tanh(0.5 * z) + 0.5


def _qkv_kernel(x_ref, w_ref, c_ref, sa_ref, sb_ref, q_ref, k_ref, v_ref):
    xb = x_ref[...].astype(BF16)
    half = ROPE_DIM // 2

    wide = lambda r: jnp.concatenate([r[...]] * (ROPE_TW // LANES), axis=1)
    ctab, satab, sbtab = wide(c_ref), wide(sa_ref), wide(sb_ref)

    def rope(a):
        return a * ctab + pltpu.roll(a, half, 1) * satab + pltpu.roll(a, ROPE_TW - half, 1) * sbtab

    def proj(col):
        return jnp.dot(xb, w_ref[:, col:col + ROPE_TW], preferred_element_type=F32)

    for c in range(Q_DIM // ROPE_TW):
        q_ref[:, c * ROPE_TW:(c + 1) * ROPE_TW] = rope(proj(c * ROPE_TW) * (HEAD_DIM ** -0.5 * LOG2E)).astype(BF16)
    for c in range(KV2_DIM // ROPE_TW):
        k_ref[:, c * ROPE_TW:(c + 1) * ROPE_TW] = rope(proj(Q_DIM + c * ROPE_TW)).astype(BF16)
        v_ref[:, c * ROPE_TW:(c + 1) * ROPE_TW] = proj(Q_DIM + KV2_DIM + c * ROPE_TW).astype(BF16)


def _qkv_proj(x, w, ctab, satab, sbtab, seq, tm):
    t = x.shape[0]
    ns = seq // tm
    n_out = Q_DIM + 2 * KV2_DIM
    tab = pl.BlockSpec((tm, LANES), lambda i: (i % ns, 0))
    return pl.pallas_call(
        _qkv_kernel,
        grid=(t // tm,),
        in_specs=[pl.BlockSpec((tm, D_MODEL), lambda i: (i, 0)), _resident((D_MODEL, n_out)), tab, tab, tab],
        out_specs=[pl.BlockSpec((tm, Q_DIM), lambda i: (i, 0)),
                   pl.BlockSpec((tm, KV2_DIM), lambda i: (i, 0)),
                   pl.BlockSpec((tm, KV2_DIM), lambda i: (i, 0))],
        out_shape=[jax.ShapeDtypeStruct((t, Q_DIM), BF16),
                   jax.ShapeDtypeStruct((t, KV2_DIM), BF16),
                   jax.ShapeDtypeStruct((t, KV2_DIM), BF16)],
        compiler_params=_cparams(("parallel",)),
        name="attn_qkv",
    )(x, w, ctab, satab, sbtab)


def _attn_kernel(seq, sink_ref, q_ref, kp_ref, kc_ref, kn_ref, vp_ref, vc_ref, vn_ref, o_ref):
    i = pl.program_id(1)
    tq = ATT_TQ
    qpos = i * tq + lax.broadcasted_iota(jnp.int32, (tq, 3 * tq), 0)
    kpos = (i - 1) * tq + lax.broadcasted_iota(jnp.int32, (tq, 3 * tq), 1)
    valid = (jnp.abs(qpos - kpos) <= WINDOW) & (kpos >= 0) & (kpos < seq)
    bias = jnp.where(valid, 0.0, -jnp.inf).astype(F32)
    bias = jnp.concatenate([bias] * GROUP, axis=0)
    first = lax.broadcasted_iota(jnp.int32, (1, LANES), 1) < HEAD_DIM
    ones_a = jnp.broadcast_to(jnp.where(first, 1.0, 0.0).astype(BF16), (3 * tq, LANES))
    ones_b = jnp.broadcast_to(jnp.where(first, 0.0, 1.0).astype(BF16), (3 * tq, LANES))
    zero = jnp.zeros((), BF16)

    def logits(hk):
        ks = slice(hk * LANES, (hk + 1) * LANES)
        k2 = jnp.concatenate([kp_ref[:, ks], kc_ref[:, ks], kn_ref[:, ks]], axis=0)
        q0 = q_ref[:, (2 * hk) * LANES:(2 * hk + 1) * LANES]
        q1 = q_ref[:, (2 * hk + 1) * LANES:(2 * hk + 2) * LANES]
        lhs = jnp.concatenate([jnp.where(first, q0, zero), jnp.where(first, q1, zero),
                               jnp.where(first, zero, q0), jnp.where(first, zero, q1)], axis=0)
        return lax.dot_general(lhs, k2, (((1,), (1,)), ((), ())), preferred_element_type=F32) + bias

    def weights(hk, s):
        heads = (4 * hk, 4 * hk + 2, 4 * hk + 1, 4 * hk + 3)
        sink = jnp.concatenate([jnp.full((tq, LANES), sink_ref[h], F32) for h in heads], axis=0)
        m = jnp.maximum(jnp.broadcast_to(jnp.max(s, axis=-1, keepdims=True), sink.shape), sink)
        p = jnp.concatenate([jnp.exp2(s[:, j * LANES:(j + 1) * LANES] - m).astype(BF16) for j in range(3)],
                            axis=1)
        return p, jnp.exp2(sink - m)

    def outputs(hk, p, es):
        ks = slice(hk * LANES, (hk + 1) * LANES)
        v2 = jnp.concatenate([vp_ref[:, ks], vc_ref[:, ks], vn_ref[:, ks]], axis=0)
        va = jnp.concatenate([jnp.where(first, v2, zero), ones_a], axis=1)
        vb = jnp.concatenate([jnp.where(first, zero, v2), ones_b], axis=1)
        out = (jnp.dot(p[:2 * tq], va, preferred_element_type=F32)
               + jnp.dot(p[2 * tq:], vb, preferred_element_type=F32))
        den = out[:, LANES:] + jnp.where(first, es[:2 * tq], es[2 * tq:])
        o = (out[:, :LANES] / den).astype(BF16)
        o_ref[:, (2 * hk) * LANES:(2 * hk + 1) * LANES] = o[:tq]
        o_ref[:, (2 * hk + 1) * LANES:(2 * hk + 2) * LANES] = o[tq:]

    scores = [logits(hk) for hk in range(N_KV_HEADS)]
    probs = [weights(hk, scores[hk]) for hk in range(N_KV_HEADS)]
    for hk in range(N_KV_HEADS):
        outputs(hk, *probs[hk])


def _attention(q, k2, v2, sink, batch, seq):
    t = q.shape[0]
    nq = seq // ATT_TQ

    def kv_spec(off):
        return pl.BlockSpec((ATT_TQ, KV2_DIM), lambda b, i: (b * nq + jnp.clip(i + off, 0, nq - 1), 0))

    return pl.pallas_call(
        functools.partial(_attn_kernel, seq),
        grid=(batch, nq),
        in_specs=[
            pl.BlockSpec(memory_space=pltpu.SMEM),
            pl.BlockSpec((ATT_TQ, Q_DIM), lambda b, i: (b * nq + i, 0)),
            kv_spec(-1), kv_spec(0), kv_spec(1), kv_spec(-1), kv_spec(0), kv_spec(1),
        ],
        out_specs=pl.BlockSpec((ATT_TQ, Q_DIM), lambda b, i: (b * nq + i, 0)),
        out_shape=jax.ShapeDtypeStruct((t, Q_DIM), BF16),
        compiler_params=_cparams(("parallel", "arbitrary")),
        name="attn_core",
    )(sink, q, k2, k2, k2, v2, v2, v2)


def _chunk_cumsum(g, rev):
    c, w = g.shape
    sub = lax.broadcasted_iota(jnp.int32, (SUBLANES, w), 0)
    groups = [g[i * SUBLANES:(i + 1) * SUBLANES] for i in range(c // SUBLANES)]
    sh = 1
    while sh < SUBLANES:
        if rev:
            groups = [x + jnp.where(sub < SUBLANES - sh, pltpu.roll(x, SUBLANES - sh, 0), 0.0) for x in groups]
        else:
            groups = [x + jnp.where(sub >= sh, pltpu.roll(x, sh, 0), 0.0) for x in groups]
        sh *= 2
    order = range(len(groups) - 1, -1, -1) if rev else range(len(groups))
    run = None
    out = [None] * len(groups)
    for i in order:
        gi = groups[i] if run is None else groups[i] + run
        out[i] = gi
        run = jnp.broadcast_to(gi[0:1] if rev else gi[SUBLANES - 1:SUBLANES], (SUBLANES, w))
    return jnp.concatenate(out, axis=0)


def _hgrn_in_kernel(layer, x_ref, w_ref, lg_ref, q_ref, kf_ref, bf_ref, kb_ref, bb_ref, v_ref, gt_ref):
    for r0 in range(0, x_ref.shape[0], HGRN_IN_TR):
        _hgrn_in_rows(layer, slice(r0, r0 + HGRN_IN_TR), x_ref, w_ref, lg_ref,
                      q_ref, kf_ref, bf_ref, kb_ref, bb_ref, v_ref, gt_ref)


def _hgrn_in_rows(layer, rws, x_ref, w_ref, lg_ref, q_ref, kf_ref, bf_ref, kb_ref, bb_ref, v_ref, gt_ref):
    xb = x_ref[rws, :].astype(BF16)
    tm = xb.shape[0]
    tn = HGRN_IN_TN

    def proj(sec, cols):
        lo = sec * D_MODEL + cols.start
        return jnp.dot(xb, w_ref[:, lo:lo + tn], preferred_element_type=F32)

    def silu(z):
        return z * _sigmoid(z)

    def put(ref, cols, val, row0=0):
        r = slice(rws.start + row0, rws.start + row0 + val.shape[0])
        for lo in range(cols.start, cols.stop, SCAN_PAIR):
            ref[lo // SCAN_PAIR, r, :] = val[:, lo - cols.start:lo - cols.start + SCAN_PAIR]

    def lower_bound(d, cols):
        rows = [lg_ref[2 * dd + d:2 * dd + d + 1, cols] for dd in range(DEPTH)]
        m = functools.reduce(jnp.maximum, rows)
        e = [jnp.exp(r - m) for r in rows]
        return functools.reduce(jnp.add, e[1:layer + 1]) / functools.reduce(jnp.add, e)

    def forget(sec, d, cols, k_ref, b_ref):
        lb = lower_bound(d, cols)
        c = 0.5 * (1.0 - lb)
        t = c * jnp.tanh(0.5 * proj(sec, cols))
        put(k_ref, cols, (c - t).astype(BF16))
        g = jnp.log2((0.5 * (1.0 + lb)) + t)
        for ci in range(tm // SCAN_C):
            rows = slice(ci * SCAN_C, (ci + 1) * SCAN_C)
            put(b_ref, cols, _chunk_cumsum(g[rows], d == 1), rows.start)

    for cc in range(D_MODEL // tn):
        cols = slice(cc * tn, (cc + 1) * tn)
        put(q_ref, cols, silu(proj(0, cols)).astype(BF16))
        forget(1, 0, cols, kf_ref, bf_ref)
        forget(2, 1, cols, kb_ref, bb_ref)
        put(v_ref, cols, proj(3, cols).astype(BF16))
        gt_ref[rws, cols] = silu(proj(4, cols)).astype(BF16)


def _hgrn_in(x, w, logits, layer, tm):
    t = x.shape[0]
    npairs = D_MODEL // SCAN_PAIR
    row = pl.BlockSpec((tm, D_MODEL), lambda i: (i, 0))
    prow = pl.BlockSpec((npairs, tm, SCAN_PAIR), lambda i: (0, i, 0))
    pbf = jax.ShapeDtypeStruct((npairs, t, SCAN_PAIR), BF16)
    pf32 = jax.ShapeDtypeStruct((npairs, t, SCAN_PAIR), F32)
    return pl.pallas_call(
        functools.partial(_hgrn_in_kernel, layer),
        grid=(t // tm,),
        in_specs=[row, _resident((D_MODEL, 5 * D_MODEL)), _resident((2 * DEPTH, D_MODEL))],
        out_specs=[prow, prow, prow, prow, prow, prow, row],
        out_shape=[pbf, pbf, pf32, pbf, pf32, pbf, jax.ShapeDtypeStruct((t, D_MODEL), BF16)],
        compiler_params=_cparams(("parallel",)),
        name="hgrn_in",
    )(x, w, logits)


def _block_diag(a0, a1):
    z = jnp.zeros_like(a0)
    return jnp.concatenate([jnp.concatenate([a0, z], axis=1), jnp.concatenate([z, a1], axis=1)], axis=0)


def _scan_consts(rev):
    c = SCAN_C
    row = lax.broadcasted_iota(jnp.int32, (c, 2 * c), 0)
    col = lax.broadcasted_iota(jnp.int32, (c, 2 * c), 1) % c
    rowk = lax.broadcasted_iota(jnp.int32, (c, SCAN_PAIR), 0)
    masks, signs = {}, {}
    half = c // 2
    while half >= SCAN_BASE:
        size = 2 * half
        q_row = ((row % size) < half) if rev else ((row % size) >= half)
        k_col = ((col % size) >= half) if rev else ((col % size) < half)
        masks[size] = ((row // size) == (col // size)) & q_row & k_col
        q_rowk = ((rowk % size) < half) if rev else ((rowk % size) >= half)
        signs[size] = jnp.where(q_rowk, 1.0, -1.0).astype(F32)
        half //= 2
    masks[0] = ((row // SCAN_BASE) == (col // SCAN_BASE)) & ((col >= row) if rev else (col <= row))
    return masks, signs


def _scan_chunk_local(q, k, v, b, rev, masks, signs):
    c = SCAN_C
    hk = HGRN_KEY
    btot = b[0:1] if rev else b[c - 1:c]

    def ref_rows(size, offset):
        parts = [jnp.broadcast_to(b[p * size + offset:p * size + offset + 1], (size, SCAN_PAIR))
                 for p in range(c // size)]
        return parts[0] if len(parts) == 1 else jnp.concatenate(parts, axis=0)

    def pair_scores(qs, ks):
        kbd = _block_diag(ks[:, :hk], ks[:, hk:])
        return lax.dot_general(qs, kbd, (((1,), (1,)), ((), ())), preferred_element_type=F32)

    scores = None
    half = c // 2
    while half >= SCAN_BASE:
        size = 2 * half
        e = jnp.exp2((b - ref_rows(size, half if rev else half - 1)) * signs[size]).astype(BF16)
        s = jnp.where(masks[size], pair_scores(q * e, k * e), 0.0)
        scores = s if scores is None else scores + s
        half //= 2
    d = b - ref_rows(SCAN_BASE, SCAN_BASE // 2)
    s = pair_scores(q * jnp.exp2(d).astype(BF16), k * jnp.exp2(-d).astype(BF16))
    scores = scores + jnp.where(masks[0], s, 0.0)

    ke = k * jnp.exp2(btot - b).astype(BF16)
    contrib = [lax.dot_general(v[:, sl], ke[:, sl], (((0,), (0,)), ((), ())), preferred_element_type=F32)
               for sl in (slice(0, hk), slice(hk, 2 * hk))]
    return q * jnp.exp2(b).astype(BF16), scores.astype(BF16), contrib, jnp.exp2(btot)


def _scan_local(q_ref, k_ref, v_ref, b_ref, rev, base, nchunks):
    masks, signs = _scan_consts(rev)
    rows = [pl.ds(base + ci * SCAN_C, SCAN_C) for ci in range(nchunks)]
    return rows, [_scan_chunk_local(q_ref[r, :], k_ref[r, :], v_ref[r, :], b_ref[r, :], rev, masks, signs)
                  for r in rows]


def _scan_serial(rows, local, v_ref, o_ref, st_ref, rev):
    hk = HGRN_KEY
    nchunks = len(rows)
    st = [st_ref[0], st_ref[1]]
    for ci in (range(nchunks - 1, -1, -1) if rev else range(nchunks)):
        qe, scores, contrib, dec = local[ci]
        v = v_ref[rows[ci], :]
        w = _block_diag(st[0].astype(BF16), st[1].astype(BF16))
        o_ref[rows[ci], :] = (
            jnp.dot(scores, _block_diag(v[:, :hk], v[:, hk:]), preferred_element_type=F32)
            + lax.dot_general(qe, w, (((1,), (1,)), ((), ())), preferred_element_type=F32)
        ).astype(o_ref.dtype)
        st = [st[0] * dec[:, :hk] + contrib[0], st[1] * dec[:, hk:] + contrib[1]]
    st_ref[0] = st[0]
    st_ref[1] = st[1]


def _scan_kernel(nsub, sub, qf_ref, qb_ref, vf_ref, vb_ref, kf_ref, bf_ref, kb_ref, bb_ref,
                 of_ref, ob_ref, stf_ref, stb_ref):
    @pl.when(pl.program_id(2) == 0)
    def _():
        stf_ref[...] = jnp.zeros_like(stf_ref)
        stb_ref[...] = jnp.zeros_like(stb_ref)

    nchunks = sub // SCAN_C

    def body(i, carry):
        fbase = pl.multiple_of(i * sub, sub)
        bbase = pl.multiple_of((nsub - 1 - i) * sub, sub)
        rows_f, local_f = _scan_local(qf_ref, kf_ref, vf_ref, bf_ref, False, fbase, nchunks)
        rows_b, local_b = _scan_local(qb_ref, kb_ref, vb_ref, bb_ref, True, bbase, nchunks)
        _scan_serial(rows_f, local_f, vf_ref, of_ref, stf_ref, False)
        _scan_serial(rows_b, local_b, vb_ref, ob_ref, stb_ref, True)
        return carry

    lax.fori_loop(0, nsub, body, 0)


def _hgrn_scan(q, kf, bf, kb, bb, v, batch, seq, blk, sub):
    npairs, t, _ = q.shape
    nb = seq // blk
    fwd = pl.BlockSpec((None, blk, SCAN_PAIR), lambda b, h, n: (h, b * nb + n, 0))
    bwd = pl.BlockSpec((None, blk, SCAN_PAIR), lambda b, h, n: (h, b * nb + nb - 1 - n, 0))
    out = jax.ShapeDtypeStruct((npairs, t, SCAN_PAIR), BF16)
    state = pltpu.VMEM((2, HGRN_KEY, HGRN_KEY), F32)
    return pl.pallas_call(
        functools.partial(_scan_kernel, blk // sub, sub),
        grid=(batch, npairs, nb),
        in_specs=[fwd, bwd, fwd, bwd, fwd, fwd, bwd, bwd],
        out_specs=[fwd, bwd],
        out_shape=[out, out],
        scratch_shapes=[state, state],
        compiler_params=_cparams(("parallel", "parallel", "arbitrary")),
        name="hgrn_scan",
    )(q, q, v, v, kf, bf, kb, bb)


def _ffn_tail(x, p_ref, wi_ref, wo_ref, g_ref, b_ref, wg_ref, wp_ref, out_ref):
    xb = x.astype(BF16)
    acc = None
    for c in range(D_FF // FFN_FC):
        lo = c * FFN_FC
        gate = jnp.dot(xb, wi_ref[:, lo:lo + FFN_FC], preferred_element_type=F32)
        up = jnp.dot(xb, wi_ref[:, D_FF + lo:D_FF + lo + FFN_FC], preferred_element_type=F32)
        h = (gate * _sigmoid(gate) * up).astype(BF16)
        y = jnp.dot(h, wo_ref[lo:lo + FFN_FC, :], preferred_element_type=F32)
        acc = y if acc is None else acc + y
    x2 = _layer_norm(ALPHA * x + acc, g_ref[...], b_ref[...])
    egate = _sigmoid(jnp.dot(x2.astype(BF16), wg_ref[...], preferred_element_type=F32))
    proj = jnp.dot(p_ref[...].astype(BF16), wp_ref[...], preferred_element_type=F32)
    out_ref[...] = x2 + egate * proj


def _attn_tail_kernel(x_ref, o_ref, wm_ref, gm_ref, bm_ref, *rest):
    y = ALPHA * x_ref[...] + jnp.dot(o_ref[...], wm_ref[...], preferred_element_type=F32)
    _ffn_tail(_layer_norm(y, gm_ref[...], bm_ref[...]), *rest)


def _hgrn_tail_kernel(x_ref, of_ref, ob_ref, gt_ref, ng_ref, wm_ref, gm_ref, bm_ref, *rest):
    ys = []
    for h in range(HGRN_HEADS):
        sl = slice(h * HGRN_KEY, (h + 1) * HGRN_KEY)
        ps = slice((h % 2) * HGRN_KEY, (h % 2 + 1) * HGRN_KEY)
        o = of_ref[h // 2, :, ps].astype(F32) + ob_ref[h // 2, :, ps].astype(F32)
        ms = jnp.mean(o * o, axis=-1, keepdims=True)
        ys.append((o * lax.rsqrt(ms + LN_EPS) * ng_ref[...] * gt_ref[:, sl].astype(F32)).astype(BF16))
    y = ALPHA * x_ref[...] + jnp.dot(jnp.concatenate(ys, axis=1), wm_ref[...], preferred_element_type=F32)
    _ffn_tail(_layer_norm(y, gm_ref[...], bm_ref[...]), *rest)


def _layer_tail(kernel_fn, name, x, mixer_inputs, mixer_specs, p, layer, w_in, w_out, g, b, w_gate, w_proj, tm):
    t = x.shape[0]
    row = pl.BlockSpec((tm, D_MODEL), lambda i: (i, 0))
    return pl.pallas_call(
        kernel_fn,
        grid=(t // tm,),
        in_specs=[row] + mixer_specs + [
            pl.BlockSpec((None, tm, PLE_DIM), lambda i: (layer, i, 0)),
            _resident((D_MODEL, 2 * D_FF)), _resident((D_FF, D_MODEL)),
            _resident((1, D_MODEL)), _resident((1, D_MODEL)),
            _resident((D_MODEL, D_MODEL)), _resident((PLE_DIM, D_MODEL))],
        out_specs=row,
        out_shape=jax.ShapeDtypeStruct((t, D_MODEL), F32),
        compiler_params=_cparams(("parallel",)),
        name=name,
    )(x, *mixer_inputs, p, w_in, w_out, g, b, w_gate, w_proj)


def _rope_tables(seq):
    half = ROPE_DIM // 2
    inv = ROPE_THETA ** (-jnp.arange(0, ROPE_DIM, 2, dtype=F32) / ROPE_DIM)
    ang = jnp.arange(seq, dtype=F32)[:, None] * inv[None, :]
    cos, sin = jnp.cos(ang), jnp.sin(ang)
    ones = jnp.ones((seq, HEAD_DIM - ROPE_DIM), F32)
    zeros = jnp.zeros((seq, half), F32)
    rest = jnp.zeros((seq, HEAD_DIM - ROPE_DIM), F32)
    reps = LANES // HEAD_DIM
    ctab = jnp.tile(jnp.concatenate([cos, cos, ones], axis=1), (1, reps))
    satab = jnp.tile(jnp.concatenate([zeros, sin, rest], axis=1), (1, reps))
    sbtab = jnp.tile(jnp.concatenate([-sin, zeros, rest], axis=1), (1, reps))
    return ctab, satab, sbtab


def _twice_per_head(w):
    d = w.shape[0]
    w = w.reshape(d, N_KV_HEADS, 1, HEAD_DIM)
    return jnp.broadcast_to(w, (d, N_KV_HEADS, 2, HEAD_DIM)).reshape(d, KV2_DIM)


def _tile(n, pref):
    while n % pref:
        pref //= 2
    return pref


def kernel(x, p, att_w_qkv, att_sink, att_w_o, hgrn_w_in, hgrn_lb_logits, hgrn_norm_g, hgrn_w_o,
           ln_mix_g, ln_mix_b, ffn_w_in, ffn_w_out, ln_ffn_g, ln_ffn_b, ple_w_gate, ple_w_proj):
    batch, seq, _ = x.shape
    t = batch * seq
    assert seq % ATT_TQ == 0
    tm = _tile(seq, ROW_TILE)
    blk = _tile(seq, SCAN_BLK)
    sub = _tile(blk, SCAN_SUB)
    assert tm % SCAN_C == 0 and sub % SCAN_C == 0
    row = pl.BlockSpec((tm, D_MODEL), lambda r: (r, 0))
    wres, vres = _resident((D_MODEL, D_MODEL)), _resident((1, D_MODEL))
    ctab, satab, sbtab = _rope_tables(seq)
    logits = hgrn_lb_logits.astype(F32).reshape(2 * DEPTH, D_MODEL)
    h = x.reshape(t, D_MODEL)
    pf = p.reshape(DEPTH, t, PLE_DIM)
    vec = lambda a: a.reshape(1, -1).astype(F32)
    for i in range(DEPTH):
        j = i // 2
        if i % 2 == 0:
            wq, wk, wv = (att_w_qkv[j][:, :Q_DIM], att_w_qkv[j][:, Q_DIM:Q_DIM + KV_DIM],
                          att_w_qkv[j][:, Q_DIM + KV_DIM:])
            w = jnp.concatenate([wq, _twice_per_head(wk), _twice_per_head(wv)], axis=1).astype(BF16)
            q, k2, v2 = _qkv_proj(h, w, ctab, satab, sbtab, seq, tm)
            o = _attention(q, k2, v2, att_sink[j].astype(F32) * LOG2E, batch, seq)
            tail = (_attn_tail_kernel, "attn_tail")
            mixer_inputs = [o, att_w_o[j].astype(BF16), vec(ln_mix_g[i]), vec(ln_mix_b[i])]
            mixer_specs = [row, wres, vres, vres]
        else:
            q, kf, bf, kb, bb, v, gate = _hgrn_in(h, hgrn_w_in[j].astype(BF16), logits, i, tm)
            of, ob = _hgrn_scan(q, kf, bf, kb, bb, v, batch, seq, blk, sub)
            tail = (_hgrn_tail_kernel, "hgrn_tail")
            mixer_inputs = [of, ob, gate, vec(hgrn_norm_g[j]), hgrn_w_o[j].astype(BF16),
                            vec(ln_mix_g[i]), vec(ln_mix_b[i])]
            prow = pl.BlockSpec((D_MODEL // SCAN_PAIR, tm, SCAN_PAIR), lambda r: (0, r, 0))
            mixer_specs = [prow, prow, row, _resident((1, HGRN_KEY)), wres, vres, vres]
        h = _layer_tail(*tail, h, mixer_inputs, mixer_specs, pf, i,
                        ffn_w_in[i].astype(BF16), ffn_w_out[i].astype(BF16),
                        vec(ln_ffn_g[i]), vec(ln_ffn_b[i]),
                        ple_w_gate[i].astype(BF16), ple_w_proj[i].astype(BF16), tm)
    return h.reshape(batch, seq, D_MODEL)
```

```python
import functools
import math

import jax
import jax.numpy as jnp
from jax import lax
from jax.experimental import pallas as pl
from jax.experimental.pallas import tpu as pltpu

F32 = jnp.float32
BF16 = jnp.bfloat16

D_MODEL = 1024
DEPTH = 4
HEAD_DIM = 64
N_Q_HEADS = 16
N_KV_HEADS = 4
GROUP = 4
Q_DIM = 1024
KV_DIM = 256
WINDOW = 128
ROPE_DIM = 16
ROPE_THETA = 500000.0
HGRN_HEADS = 8
HGRN_KEY = 128
D_FF = 2816
PLE_DIM = 256
ALPHA = (2 * DEPTH) ** 0.25
LN_EPS = 1e-5
LOG2E = math.log2(math.e)

LANES = 128
SUBLANES = 8
VMEM_LIMIT = 56 * 1024 * 1024

ROW_TILE = 512
ATT_TQ = 128
KV2_DIM = 2 * KV_DIM
ROPE_TW = 256
FFN_FC = 2816
SCAN_C = 64
SCAN_BASE = 16
SCAN_PAIR = 2 * HGRN_KEY
SCAN_SUB = 1024
SCAN_BLK = 1024
HGRN_IN_TN = 1024
HGRN_IN_TR = 512


def _cparams(sem):
    return pltpu.CompilerParams(dimension_semantics=sem, vmem_limit_bytes=VMEM_LIMIT)


def _resident(shape):
    return pl.BlockSpec(shape, lambda *_: (0,) * len(shape), pipeline_mode=pl.Buffered(1))


def _layer_norm(y, g, b):
    mu = jnp.mean(y, axis=-1, keepdims=True)
    d = y - mu
    var = jnp.mean(d * d, axis=-1, keepdims=True)
    return d * lax.rsqrt(var + LN_EPS) * g + b


def _sigmoid(z):
    return 0.5 * jnp.tanh(0.5 * z) + 0.5


def _qkv_kernel(x_ref, w_ref, c_ref, sa_ref, sb_ref, q_ref, k_ref, v_ref):
    xb = x_ref[...].astype(BF16)
    half = ROPE_DIM // 2

    wide = lambda r: jnp.concatenate([r[...]] * (ROPE_TW // LANES), axis=1)
    ctab, satab, sbtab = wide(c_ref), wide(sa_ref), wide(sb_ref)

    def rope(a):
        return a * ctab + pltpu.roll(a, half, 1) * satab + pltpu.roll(a, ROPE_TW - half, 1) * sbtab

    def proj(col):
        return jnp.dot(xb, w_ref[:, col:col + ROPE_TW], preferred_element_type=F32)

    for c in range(Q_DIM // ROPE_TW):
        q_ref[:, c * ROPE_TW:(c + 1) * ROPE_TW] = rope(proj(c * ROPE_TW) * (HEAD_DIM ** -0.5 * LOG2E)).astype(BF16)
    for c in range(KV2_DIM // ROPE_TW):
        k_ref[:, c * ROPE_TW:(c + 1) * ROPE_TW] = rope(proj(Q_DIM + c * ROPE_TW)).astype(BF16)
        v_ref[:, c * ROPE_TW:(c + 1) * ROPE_TW] = proj(Q_DIM + KV2_DIM + c * ROPE_TW).astype(BF16)


def _qkv_proj(x, w, ctab, satab, sbtab, seq, tm):
    t = x.shape[0]
    ns = seq // tm
    n_out = Q_DIM + 2 * KV2_DIM
    tab = pl.BlockSpec((tm, LANES), lambda i: (i % ns, 0))
    return pl.pallas_call(
        _qkv_kernel,
        grid=(t // tm,),
        in_specs=[pl.BlockSpec((tm, D_MODEL), lambda i: (i, 0)), _resident((D_MODEL, n_out)), tab, tab, tab],
        out_specs=[pl.BlockSpec((tm, Q_DIM), lambda i: (i, 0)),
                   pl.BlockSpec((tm, KV2_DIM), lambda i: (i, 0)),
                   pl.BlockSpec((tm, KV2_DIM), lambda i: (i, 0))],
        out_shape=[jax.ShapeDtypeStruct((t, Q_DIM), BF16),
                   jax.ShapeDtypeStruct((t, KV2_DIM), BF16),
                   jax.ShapeDtypeStruct((t, KV2_DIM), BF16)],
        compiler_params=_cparams(("parallel",)),
        name="attn_qkv",
    )(x, w, ctab, satab, sbtab)


def _attn_kernel(seq, sink_ref, q_ref, kp_ref, kc_ref, kn_ref, vp_ref, vc_ref, vn_ref, o_ref):
    i = pl.program_id(1)
    tq = ATT_TQ
    qpos = i * tq + lax.broadcasted_iota(jnp.int32, (tq, 3 * tq), 0)
    kpos = (i - 1) * tq + lax.broadcasted_iota(jnp.int32, (tq, 3 * tq), 1)
    valid = (jnp.abs(qpos - kpos) <= WINDOW) & (kpos >= 0) & (kpos < seq)
    bias = jnp.where(valid, 0.0, -jnp.inf).astype(F32)
    bias = jnp.concatenate([bias] * GROUP, axis=0)
    first = lax.broadcasted_iota(jnp.int32, (1, LANES), 1) < HEAD_DIM
    ones_a = jnp.broadcast_to(jnp.where(first, 1.0, 0.0).astype(BF16), (3 * tq, LANES))
    ones_b = jnp.broadcast_to(jnp.where(first, 0.0, 1.0).astype(BF16), (3 * tq, LANES))
    zero = jnp.zeros((), BF16)

    def logits(hk):
        ks = slice(hk * LANES, (hk + 1) * LANES)
        k2 = jnp.concatenate([kp_ref[:, ks], kc_ref[:, ks], kn_ref[:, ks]], axis=0)
        q0 = q_ref[:, (2 * hk) * LANES:(2 * hk + 1) * LANES]
        q1 = q_ref[:, (2 * hk + 1) * LANES:(2 * hk + 2) * LANES]
        lhs = jnp.concatenate([jnp.where(first, q0, zero), jnp.where(first, q1, zero),
                               jnp.where(first, zero, q0), jnp.where(first, zero, q1)], axis=0)
        return lax.dot_general(lhs, k2, (((1,), (1,)), ((), ())), preferred_element_type=F32) + bias

    def weights(hk, s):
        heads = (4 * hk, 4 * hk + 2, 4 * hk + 1, 4 * hk + 3)
        sink = jnp.concatenate([jnp.full((tq, LANES), sink_ref[h], F32) for h in heads], axis=0)
        m = jnp.maximum(jnp.broadcast_to(jnp.max(s, axis=-1, keepdims=True), sink.shape), sink)
        p = jnp.concatenate([jnp.exp2(s[:, j * LANES:(j + 1) * LANES] - m).astype(BF16) for j in range(3)],
                            axis=1)
        return p, jnp.exp2(sink - m)

    def outputs(hk, p, es):
        ks = slice(hk * LANES, (hk + 1) * LANES)
        v2 = jnp.concatenate([vp_ref[:, ks], vc_ref[:, ks], vn_ref[:, ks]], axis=0)
        va = jnp.concatenate([jnp.where(first, v2, zero), ones_a], axis=1)
        vb = jnp.concatenate([jnp.where(first, zero, v2), ones_b], axis=1)
        out = (jnp.dot(p[:2 * tq], va, preferred_element_type=F32)
               + jnp.dot(p[2 * tq:], vb, preferred_element_type=F32))
        den = out[:, LANES:] + jnp.where(first, es[:2 * tq], es[2 * tq:])
        o = (out[:, :LANES] / den).astype(BF16)
        o_ref[:, (2 * hk) * LANES:(2 * hk + 1) * LANES] = o[:tq]
        o_ref[:, (2 * hk + 1) * LANES:(2 * hk + 2) * LANES] = o[tq:]

    scores = [logits(hk) for hk in range(N_KV_HEADS)]
    probs = [weights(hk, scores[hk]) for hk in range(N_KV_HEADS)]
    for hk in range(N_KV_HEADS):
        outputs(hk, *probs[hk])


def _attention(q, k2, v2, sink, batch, seq):
    t = q.shape[0]
    nq = seq // ATT_TQ

    def kv_spec(off):
        return pl.BlockSpec((ATT_TQ, KV2_DIM), lambda b, i: (b * nq + jnp.clip(i + off, 0, nq - 1), 0))

    return pl.pallas_call(
        functools.partial(_attn_kernel, seq),
        grid=(batch, nq),
        in_specs=[
            pl.BlockSpec(memory_space=pltpu.SMEM),
            pl.BlockSpec((ATT_TQ, Q_DIM), lambda b, i: (b * nq + i, 0)),
            kv_spec(-1), kv_spec(0), kv_spec(1), kv_spec(-1), kv_spec(0), kv_spec(1),
        ],
        out_specs=pl.BlockSpec((ATT_TQ, Q_DIM), lambda b, i: (b * nq + i, 0)),
        out_shape=jax.ShapeDtypeStruct((t, Q_DIM), BF16),
        compiler_params=_cparams(("parallel", "arbitrary")),
        name="attn_core",
    )(sink, q, k2, k2, k2, v2, v2, v2)


def _chunk_cumsum(g, rev):
    c, w = g.shape
    sub = lax.broadcasted_iota(jnp.int32, (SUBLANES, w), 0)
    groups = [g[i * SUBLANES:(i + 1) * SUBLANES] for i in range(c // SUBLANES)]
    sh = 1
    while sh < SUBLANES:
        if rev:
            groups = [x + jnp.where(sub < SUBLANES - sh, pltpu.roll(x, SUBLANES - sh, 0), 0.0) for x in groups]
        else:
            groups = [x + jnp.where(sub >= sh, pltpu.roll(x, sh, 0), 0.0) for x in groups]
        sh *= 2
    order = range(len(groups) - 1, -1, -1) if rev else range(len(groups))
    run = None
    out = [None] * len(groups)
    for i in order:
        gi = groups[i] if run is None else groups[i] + run
        out[i] = gi
        run = jnp.broadcast_to(gi[0:1] if rev else gi[SUBLANES - 1:SUBLANES], (SUBLANES, w))
    return jnp.concatenate(out, axis=0)


def _hgrn_in_kernel(layer, x_ref, w_ref, lg_ref, q_ref, kf_ref, bf_ref, kb_ref, bb_ref, v_ref, gt_ref):
    for r0 in range(0, x_ref.shape[0], HGRN_IN_TR):
        _hgrn_in_rows(layer, slice(r0, r0 + HGRN_IN_TR), x_ref, w_ref, lg_ref,
                      q_ref, kf_ref, bf_ref, kb_ref, bb_ref, v_ref, gt_ref)


def _hgrn_in_rows(layer, rws, x_ref, w_ref, lg_ref, q_ref, kf_ref, bf_ref, kb_ref, bb_ref, v_ref, gt_ref):
    xb = x_ref[rws, :].astype(BF16)
    tm = xb.shape[0]
    tn = HGRN_IN_TN

    def proj(sec, cols):
        lo = sec * D_MODEL + cols.start
        return jnp.dot(xb, w_ref[:, lo:lo + tn], preferred_element_type=F32)

    def silu(z):
        return z * _sigmoid(z)

    def put(ref, cols, val, row0=0):
        r = slice(rws.start + row0, rws.start + row0 + val.shape[0])
        for lo in range(cols.start, cols.stop, SCAN_PAIR):
            ref[lo // SCAN_PAIR, r, :] = val[:, lo - cols.start:lo - cols.start + SCAN_PAIR]

    def lower_bound(d, cols):
        rows = [lg_ref[2 * dd + d:2 * dd + d + 1, cols] for dd in range(DEPTH)]
        m = functools.reduce(jnp.maximum, rows)
        e = [jnp.exp(r - m) for r in rows]
        return functools.reduce(jnp.add, e[1:layer + 1]) / functools.reduce(jnp.add, e)

    def forget(sec, d, cols, k_ref, b_ref):
        lb = lower_bound(d, cols)
        c = 0.5 * (1.0 - lb)
        t = c * jnp.tanh(0.5 * proj(sec, cols))
        put(k_ref, cols, (c - t).astype(BF16))
        g = jnp.log2((0.5 * (1.0 + lb)) + t)
        for ci in range(tm // SCAN_C):
            rows = slice(ci * SCAN_C, (ci + 1) * SCAN_C)
            put(b_ref, cols, _chunk_cumsum(g[rows], d == 1), rows.start)

    for cc in range(D_MODEL // tn):
        cols = slice(cc * tn, (cc + 1) * tn)
        put(q_ref, cols, silu(proj(0, cols)).astype(BF16))
        forget(1, 0, cols, kf_ref, bf_ref)
        forget(2, 1, cols, kb_ref, bb_ref)
        put(v_ref, cols, proj(3, cols).astype(BF16))
        gt_ref[rws, cols] = silu(proj(4, cols)).astype(BF16)


def _hgrn_in(x, w, logits, layer, tm):
    t = x.shape[0]
    npairs = D_MODEL // SCAN_PAIR
    row = pl.BlockSpec((tm, D_MODEL), lambda i: (i, 0))
    prow = pl.BlockSpec((npairs, tm, SCAN_PAIR), lambda i: (0, i, 0))
    pbf = jax.ShapeDtypeStruct((npairs, t, SCAN_PAIR), BF16)
    pf32 = jax.ShapeDtypeStruct((npairs, t, SCAN_PAIR), F32)
    return pl.pallas_call(
        functools.partial(_hgrn_in_kernel, layer),
        grid=(t // tm,),
        in_specs=[row, _resident((D_MODEL, 5 * D_MODEL)), _resident((2 * DEPTH, D_MODEL))],
        out_specs=[prow, prow, prow, prow, prow, prow, row],
        out_shape=[pbf, pbf, pf32, pbf, pf32, pbf, jax.ShapeDtypeStruct((t, D_MODEL), BF16)],
        compiler_params=_cparams(("parallel",)),
        name="hgrn_in",
    )(x, w, logits)


def _block_diag(a0, a1):
    z = jnp.zeros_like(a0)
    return jnp.concatenate([jnp.concatenate([a0, z], axis=1), jnp.concatenate([z, a1], axis=1)], axis=0)


def _scan_consts(rev):
    c = SCAN_C
    row = lax.broadcasted_iota(jnp.int32, (c, 2 * c), 0)
    col = lax.broadcasted_iota(jnp.int32, (c, 2 * c), 1) % c
    rowk = lax.broadcasted_iota(jnp.int32, (c, SCAN_PAIR), 0)
    masks, signs = {}, {}
    half = c // 2
    while half >= SCAN_BASE:
        size = 2 * half
        q_row = ((row % size) < half) if rev else ((row % size) >= half)
        k_col = ((col % size) >= half) if rev else ((col % size) < half)
        masks[size] = ((row // size) == (col // size)) & q_row & k_col
        q_rowk = ((rowk % size) < half) if rev else ((rowk % size) >= half)
        signs[size] = jnp.where(q_rowk, 1.0, -1.0).astype(F32)
        half //= 2
    masks[0] = ((row // SCAN_BASE) == (col // SCAN_BASE)) & ((col >= row) if rev else (col <= row))
    return masks, signs


def _scan_chunk_local(q, k, v, b, rev, masks, signs):
    c = SCAN_C
    hk = HGRN_KEY
    btot = b[0:1] if rev else b[c - 1:c]

    def ref_rows(size, offset):
        parts = [jnp.broadcast_to(b[p * size + offset:p * size + offset + 1], (size, SCAN_PAIR))
                 for p in range(c // size)]
        return parts[0] if len(parts) == 1 else jnp.concatenate(parts, axis=0)

    def pair_scores(qs, ks):
        kbd = _block_diag(ks[:, :hk], ks[:, hk:])
        return lax.dot_general(qs, kbd, (((1,), (1,)), ((), ())), preferred_element_type=F32)

    scores = None
    half = c // 2
    while half >= SCAN_BASE:
        size = 2 * half
        e = jnp.exp2((b - ref_rows(size, half if rev else half - 1)) * signs[size]).astype(BF16)
        s = jnp.where(masks[size], pair_scores(q * e, k * e), 0.0)
        scores = s if scores is None else scores + s
        half //= 2
    d = b - ref_rows(SCAN_BASE, SCAN_BASE // 2)
    s = pair_scores(q * jnp.exp2(d).astype(BF16), k * jnp.exp2(-d).astype(BF16))
    scores = scores + jnp.where(masks[0], s, 0.0)

    ke = k * jnp.exp2(btot - b).astype(BF16)
    contrib = [lax.dot_general(v[:, sl], ke[:, sl], (((0,), (0,)), ((), ())), preferred_element_type=F32)
               for sl in (slice(0, hk), slice(hk, 2 * hk))]
    return q * jnp.exp2(b).astype(BF16), scores.astype(BF16), contrib, jnp.exp2(btot)


def _scan_local(q_ref, k_ref, v_ref, b_ref, rev, base, nchunks):
    masks, signs = _scan_consts(rev)
    rows = [pl.ds(base + ci * SCAN_C, SCAN_C) for ci in range(nchunks)]
    return rows, [_scan_chunk_local(q_ref[r, :], k_ref[r, :], v_ref[r, :], b_ref[r, :], rev, masks, signs)
                  for r in rows]


def _scan_serial(rows, local, v_ref, o_ref, st_ref, rev):
    hk = HGRN_KEY
    nchunks = len(rows)
    st = [st_ref[0], st_ref[1]]
    for ci in (range(nchunks - 1, -1, -1) if rev else range(nchunks)):
        qe, scores, contrib, dec = local[ci]
        v = v_ref[rows[ci], :]
        w = _block_diag(st[0].astype(BF16), st[1].astype(BF16))
        o_ref[rows[ci], :] = (
            jnp.dot(scores, _block_diag(v[:, :hk], v[:, hk:]), preferred_element_type=F32)
            + lax.dot_general(qe, w, (((1,), (1,)), ((), ())), preferred_element_type=F32)
        ).astype(o_ref.dtype)
        st = [st[0] * dec[:, :hk] + contrib[0], st[1] * dec[:, hk:] + contrib[1]]
    st_ref[0] = st[0]
    st_ref[1] = st[1]


def _scan_kernel(nsub, sub, qf_ref, qb_ref, vf_ref, vb_ref, kf_ref, bf_ref, kb_ref, bb_ref,
                 of_ref, ob_ref, stf_ref, stb_ref):
    @pl.when(pl.program_id(2) == 0)
    def _():
        stf_ref[...] = jnp.zeros_like(stf_ref)
        stb_ref[...] = jnp.zeros_like(stb_ref)

    nchunks = sub // SCAN_C

    def body(i, carry):
        fbase = pl.multiple_of(i * sub, sub)
        bbase = pl.multiple_of((nsub - 1 - i) * sub, sub)
        rows_f, local_f = _scan_local(qf_ref, kf_ref, vf_ref, bf_ref, False, fbase, nchunks)
        rows_b, local_b = _scan_local(qb_ref, kb_ref, vb_ref, bb_ref, True, bbase, nchunks)
        _scan_serial(rows_f, local_f, vf_ref, of_ref, stf_ref, False)
        _scan_serial(rows_b, local_b, vb_ref, ob_ref, stb_ref, True)
        return carry

    lax.fori_loop(0, nsub, body, 0)


def _hgrn_scan(q, kf, bf, kb, bb, v, batch, seq, blk, sub):
    npairs, t, _ = q.shape
    nb = seq // blk
    fwd = pl.BlockSpec((None, blk, SCAN_PAIR), lambda b, h, n: (h, b * nb + n, 0))
    bwd = pl.BlockSpec((None, blk, SCAN_PAIR), lambda b, h, n: (h, b * nb + nb - 1 - n, 0))
    out = jax.ShapeDtypeStruct((npairs, t, SCAN_PAIR), BF16)
    state = pltpu.VMEM((2, HGRN_KEY, HGRN_KEY), F32)
    return pl.pallas_call(
        functools.partial(_scan_kernel, blk // sub, sub),
        grid=(batch, npairs, nb),
        in_specs=[fwd, bwd, fwd, bwd, fwd, fwd, bwd, bwd],
        out_specs=[fwd, bwd],
        out_shape=[out, out],
        scratch_shapes=[state, state],
        compiler_params=_cparams(("parallel", "parallel", "arbitrary")),
        name="hgrn_scan",
    )(q, q, v, v, kf, bf, kb, bb)


def _ffn_tail(x, p_ref, wi_ref, wo_ref, g_ref, b_ref, wg_ref, wp_ref, out_ref):
    xb = x.astype(BF16)
    acc = None
    for c in range(D_FF // FFN_FC):
        lo = c * FFN_FC
        gate = jnp.dot(xb, wi_ref[:, lo:lo + FFN_FC], preferred_element_type=F32)
        up = jnp.dot(xb, wi_ref[:, D_FF + lo:D_FF + lo + FFN_FC], preferred_element_type=F32)
        h = (gate * _sigmoid(gate) * up).astype(BF16)
        y = jnp.dot(h, wo_ref[lo:lo + FFN_FC, :], preferred_element_type=F32)
        acc = y if acc is None else acc + y
    x2 = _layer_norm(ALPHA * x + acc, g_ref[...], b_ref[...])
    egate = _sigmoid(jnp.dot(x2.astype(BF16), wg_ref[...], preferred_element_type=F32))
    proj = jnp.dot(p_ref[...].astype(BF16), wp_ref[...], preferred_element_type=F32)
    out_ref[...] = x2 + egate * proj


def _attn_tail_kernel(x_ref, o_ref, wm_ref, gm_ref, bm_ref, *rest):
    y = ALPHA * x_ref[...] + jnp.dot(o_ref[...], wm_ref[...], preferred_element_type=F32)
    _ffn_tail(_layer_norm(y, gm_ref[...], bm_ref[...]), *rest)


def _hgrn_tail_kernel(x_ref, of_ref, ob_ref, gt_ref, ng_ref, wm_ref, gm_ref, bm_ref, *rest):
    ys = []
    for h in range(HGRN_HEADS):
        sl = slice(h * HGRN_KEY, (h + 1) * HGRN_KEY)
        ps = slice((h % 2) * HGRN_KEY, (h % 2 + 1) * HGRN_KEY)
        o = of_ref[h // 2, :, ps].astype(F32) + ob_ref[h // 2, :, ps].astype(F32)
        ms = jnp.mean(o * o, axis=-1, keepdims=True)
        ys.append((o * lax.rsqrt(ms + LN_EPS) * ng_ref[...] * gt_ref[:, sl].astype(F32)).astype(BF16))
    y = ALPHA * x_ref[...] + jnp.dot(jnp.concatenate(ys, axis=1), wm_ref[...], preferred_element_type=F32)
    _ffn_tail(_layer_norm(y, gm_ref[...], bm_ref[...]), *rest)


def _layer_tail(kernel_fn, name, x, mixer_inputs, mixer_specs, p, layer, w_in, w_out, g, b, w_gate, w_proj, tm):
    t = x.shape[0]
    row = pl.BlockSpec((tm, D_MODEL), lambda i: (i, 0))
    return pl.pallas_call(
        kernel_fn,
        grid=(t // tm,),
        in_specs=[row] + mixer_specs + [
            pl.BlockSpec((None, tm, PLE_DIM), lambda i: (layer, i, 0)),
            _resident((D_MODEL, 2 * D_FF)), _resident((D_FF, D_MODEL)),
            _resident((1, D_MODEL)), _resident((1, D_MODEL)),
            _resident((D_MODEL, D_MODEL)), _resident((PLE_DIM, D_MODEL))],
        out_specs=row,
        out_shape=jax.ShapeDtypeStruct((t, D_MODEL), F32),
        compiler_params=_cparams(("parallel",)),
        name=name,
    )(x, *mixer_inputs, p, w_in, w_out, g, b, w_gate, w_proj)


def _rope_tables(seq):
    half = ROPE_DIM // 2
    inv = ROPE_THETA ** (-jnp.arange(0, ROPE_DIM, 2, dtype=F32) / ROPE_DIM)
    ang = jnp.arange(seq, dtype=F32)[:, None] * inv[None, :]
    cos, sin = jnp.cos(ang), jnp.sin(ang)
    ones = jnp.ones((seq, HEAD_DIM - ROPE_DIM), F32)
    zeros = jnp.zeros((seq, half), F32)
    rest = jnp.zeros((seq, HEAD_DIM - ROPE_DIM), F32)
    reps = LANES // HEAD_DIM
    ctab = jnp.tile(jnp.concatenate([cos, cos, ones], axis=1), (1, reps))
    satab = jnp.tile(jnp.concatenate([zeros, sin, rest], axis=1), (1, reps))
    sbtab = jnp.tile(jnp.concatenate([-sin, zeros, rest], axis=1), (1, reps))
    return ctab, satab, sbtab


def _twice_per_head(w):
    d = w.shape[0]
    w = w.reshape(d, N_KV_HEADS, 1, HEAD_DIM)
    return jnp.broadcast_to(w, (d, N_KV_HEADS, 2, HEAD_DIM)).reshape(d, KV2_DIM)


def _tile(n, pref):
    while n % pref:
        pref //= 2
    return pref


def kernel(x, p, att_w_qkv, att_sink, att_w_o, hgrn_w_in, hgrn_lb_logits, hgrn_norm_g, hgrn_w_o,
           ln_mix_g, ln_mix_b, ffn_w_in, ffn_w_out, ln_ffn_g, ln_ffn_b, ple_w_gate, ple_w_proj):
    batch, seq, _ = x.shape
    t = batch * seq
    assert seq % ATT_TQ == 0
    tm = _tile(seq, ROW_TILE)
    blk = _tile(seq, SCAN_BLK)
    sub = _tile(blk, SCAN_SUB)
    assert tm % SCAN_C == 0 and sub % SCAN_C == 0
    row = pl.BlockSpec((tm, D_MODEL), lambda r: (r, 0))
    wres, vres = _resident((D_MODEL, D_MODEL)), _resident((1, D_MODEL))
    ctab, satab, sbtab = _rope_tables(seq)
    logits = hgrn_lb_logits.astype(F32).reshape(2 * DEPTH, D_MODEL)
    h = x.reshape(t, D_MODEL)
    pf = p.reshape(DEPTH, t, PLE_DIM)
    vec = lambda a: a.reshape(1, -1).astype(F32)
    for i in range(DEPTH):
        j = i // 2
        if i % 2 == 0:
            wq, wk, wv = (att_w_qkv[j][:, :Q_DIM], att_w_qkv[j][:, Q_DIM:Q_DIM + KV_DIM],
                          att_w_qkv[j][:, Q_DIM + KV_DIM:])
            w = jnp.concatenate([wq, _twice_per_head(wk), _twice_per_head(wv)], axis=1).astype(BF16)
            q, k2, v2 = _qkv_proj(h, w, ctab, satab, sbtab, seq, tm)
            o = _attention(q, k2, v2, att_sink[j].astype(F32) * LOG2E, batch, seq)
            tail = (_attn_tail_kernel, "attn_tail")
            mixer_inputs = [o, att_w_o[j].astype(BF16), vec(ln_mix_g[i]), vec(ln_mix_b[i])]
            mixer_specs = [row, wres, vres, vres]
        else:
            q, kf, bf, kb, bb, v, gate = _hgrn_in(h, hgrn_w_in[j].astype(BF16), logits, i, tm)
            of, ob = _hgrn_scan(q, kf, bf, kb, bb, v, batch, seq, blk, sub)
            tail = (_hgrn_tail_kernel, "hgrn_tail")
            mixer_inputs = [of, ob, gate, vec(hgrn_norm_g[j]), hgrn_w_o[j].astype(BF16),
                            vec(ln_mix_g[i]), vec(ln_mix_b[i])]
            prow = pl.BlockSpec((D_MODEL // SCAN_PAIR, tm, SCAN_PAIR), lambda r: (0, r, 0))
            mixer_specs = [prow, prow, row, _resident((1, HGRN_KEY)), wres, vres, vres]
        h = _layer_tail(*tail, h, mixer_inputs, mixer_specs, pf, i,
                        ffn_w_in[i].astype(BF16), ffn_w_out[i].astype(BF16),
                        vec(ln_ffn_g[i]), vec(ln_ffn_b[i]),
                        ple_w_gate[i].astype(BF16), ple_w_proj[i].astype(BF16), tm)
    return h.reshape(batch, seq, D_MODEL)
```

```python
import functools
import math

import jax
import jax.numpy as jnp
from jax import lax
from jax.experimental import pallas as pl
from jax.experimental.pallas import tpu as pltpu

F32 = jnp.float32
BF16 = jnp.bfloat16

D_MODEL = 1024
DEPTH = 4
HEAD_DIM = 64
N_Q_HEADS = 16
N_KV_HEADS = 4
GROUP = 4
Q_DIM = 1024
KV_DIM = 256
WINDOW = 128
ROPE_DIM = 16
ROPE_THETA = 500000.0
HGRN_HEADS = 8
HGRN_KEY = 128
D_FF = 2816
PLE_DIM = 256
ALPHA = (2 * DEPTH) ** 0.25
LN_EPS = 1e-5
LOG2E = math.log2(math.e)

LANES = 128
SUBLANES = 8
VMEM_LIMIT = 56 * 1024 * 1024

ROW_TILE = 512
ATT_TQ = 128
KV2_DIM = 2 * KV_DIM
ROPE_TW = 256
FFN_FC = 2816
SCAN_C = 64
SCAN_BASE = 16
SCAN_PAIR = 2 * HGRN_KEY
SCAN_SUB = 1024
SCAN_BLK = 2048
HGRN_IN_TN = 1024
HGRN_IN_TR = 512


def _cparams(sem):
    return pltpu.CompilerParams(dimension_semantics=sem, vmem_limit_bytes=VMEM_LIMIT)


def _resident(shape):
    return pl.BlockSpec(shape, lambda *_: (0,) * len(shape), pipeline_mode=pl.Buffered(1))


def _layer_norm(y, g, b):
    mu = jnp.mean(y, axis=-1, keepdims=True)
    d = y - mu
    var = jnp.mean(d * d, axis=-1, keepdims=True)
    return d * lax.rsqrt(var + LN_EPS) * g + b


def _sigmoid(z):
    return 0.5 * jnp.tanh(0.5 * z) + 0.5


def _qkv_kernel(x_ref, w_ref, c_ref, sa_ref, sb_ref, q_ref, k_ref, v_ref):
    xb = x_ref[...].astype(BF16)
    half = ROPE_DIM // 2

    wide = lambda r: jnp.concatenate([r[...]] * (ROPE_TW // LANES), axis=1)
    ctab, satab, sbtab = wide(c_ref), wide(sa_ref), wide(sb_ref)

    def rope(a):
        return a * ctab + pltpu.roll(a, half, 1) * satab + pltpu.roll(a, ROPE_TW - half, 1) * sbtab

    def proj(col):
        return jnp.dot(xb, w_ref[:, col:col + ROPE_TW], preferred_element_type=F32)

    for c in range(Q_DIM // ROPE_TW):
        q_ref[:, c * ROPE_TW:(c + 1) * ROPE_TW] = rope(proj(c * ROPE_TW) * (HEAD_DIM ** -0.5 * LOG2E)).astype(BF16)
    for c in range(KV2_DIM // ROPE_TW):
        k_ref[:, c * ROPE_TW:(c + 1) * ROPE_TW] = rope(proj(Q_DIM + c * ROPE_TW)).astype(BF16)
        v_ref[:, c * ROPE_TW:(c + 1) * ROPE_TW] = proj(Q_DIM + KV2_DIM + c * ROPE_TW).astype(BF16)


def _qkv_proj(x, w, ctab, satab, sbtab, seq, tm):
    t = x.shape[0]
    ns = seq // tm
    n_out = Q_DIM + 2 * KV2_DIM
    tab = pl.BlockSpec((tm, LANES), lambda i: (i % ns, 0))
    return pl.pallas_call(
        _qkv_kernel,
        grid=(t // tm,),
        in_specs=[pl.BlockSpec((tm, D_MODEL), lambda i: (i, 0)), _resident((D_MODEL, n_out)), tab, tab, tab],
        out_specs=[pl.BlockSpec((tm, Q_DIM), lambda i: (i, 0)),
                   pl.BlockSpec((tm, KV2_DIM), lambda i: (i, 0)),
                   pl.BlockSpec((tm, KV2_DIM), lambda i: (i, 0))],
        out_shape=[jax.ShapeDtypeStruct((t, Q_DIM), BF16),
                   jax.ShapeDtypeStruct((t, KV2_DIM), BF16),
                   jax.ShapeDtypeStruct((t, KV2_DIM), BF16)],
        compiler_params=_cparams(("parallel",)),
        name="attn_qkv",
    )(x, w, ctab, satab, sbtab)


def _attn_kernel(seq, sink_ref, q_ref, kp_ref, kc_ref, kn_ref, vp_ref, vc_ref, vn_ref, o_ref):
    i = pl.program_id(1)
    tq = ATT_TQ
    qpos = i * tq + lax.broadcasted_iota(jnp.int32, (tq, 3 * tq), 0)
    kpos = (i - 1) * tq + lax.broadcasted_iota(jnp.int32, (tq, 3 * tq), 1)
    valid = (jnp.abs(qpos - kpos) <= WINDOW) & (kpos >= 0) & (kpos < seq)
    bias = jnp.where(valid, 0.0, -jnp.inf).astype(F32)
    bias = jnp.concatenate([bias] * GROUP, axis=0)
    first = lax.broadcasted_iota(jnp.int32, (1, LANES), 1) < HEAD_DIM
    ones_a = jnp.broadcast_to(jnp.where(first, 1.0, 0.0).astype(BF16), (3 * tq, LANES))
    ones_b = jnp.broadcast_to(jnp.where(first, 0.0, 1.0).astype(BF16), (3 * tq, LANES))
    zero = jnp.zeros((), BF16)

    def logits(hk):
        ks = slice(hk * LANES, (hk + 1) * LANES)
        k2 = jnp.concatenate([kp_ref[:, ks], kc_ref[:, ks], kn_ref[:, ks]], axis=0)
        q0 = q_ref[:, (2 * hk) * LANES:(2 * hk + 1) * LANES]
        q1 = q_ref[:, (2 * hk + 1) * LANES:(2 * hk + 2) * LANES]
        lhs = jnp.concatenate([jnp.where(first, q0, zero), jnp.where(first, q1, zero),
                               jnp.where(first, zero, q0), jnp.where(first, zero, q1)], axis=0)
        return lax.dot_general(lhs, k2, (((1,), (1,)), ((), ())), preferred_element_type=F32) + bias

    def weights(hk, s):
        heads = (4 * hk, 4 * hk + 2, 4 * hk + 1, 4 * hk + 3)
        sink = jnp.concatenate([jnp.full((tq, LANES), sink_ref[h], F32) for h in heads], axis=0)
        m = jnp.maximum(jnp.broadcast_to(jnp.max(s, axis=-1, keepdims=True), sink.shape), sink)
        p = jnp.concatenate([jnp.exp2(s[:, j * LANES:(j + 1) * LANES] - m).astype(BF16) for j in range(3)],
                            axis=1)
        return p, jnp.exp2(sink - m)

    def outputs(hk, p, es):
        ks = slice(hk * LANES, (hk + 1) * LANES)
        v2 = jnp.concatenate([vp_ref[:, ks], vc_ref[:, ks], vn_ref[:, ks]], axis=0)
        va = jnp.concatenate([jnp.where(first, v2, zero), ones_a], axis=1)
        vb = jnp.concatenate([jnp.where(first, zero, v2), ones_b], axis=1)
        out = (jnp.dot(p[:2 * tq], va, preferred_element_type=F32)
               + jnp.dot(p[2 * tq:], vb, preferred_element_type=F32))
        den = out[:, LANES:] + jnp.where(first, es[:2 * tq], es[2 * tq:])
        o = (out[:, :LANES] / den).astype(BF16)
        o_ref[:, (2 * hk) * LANES:(2 * hk + 1) * LANES] = o[:tq]
        o_ref[:, (2 * hk + 1) * LANES:(2 * hk + 2) * LANES] = o[tq:]

    scores = [logits(hk) for hk in range(N_KV_HEADS)]
    probs = [weights(hk, scores[hk]) for hk in range(N_KV_HEADS)]
    for hk in range(N_KV_HEADS):
        outputs(hk, *probs[hk])


def _attention(q, k2, v2, sink, batch, seq):
    t = q.shape[0]
    nq = seq // ATT_TQ

    def kv_spec(off):
        return pl.BlockSpec((ATT_TQ, KV2_DIM), lambda b, i: (b * nq + jnp.clip(i + off, 0, nq - 1), 0))

    return pl.pallas_call(
        functools.partial(_attn_kernel, seq),
        grid=(batch, nq),
        in_specs=[
            pl.BlockSpec(memory_space=pltpu.SMEM),
            pl.BlockSpec((ATT_TQ, Q_DIM), lambda b, i: (b * nq + i, 0)),
            kv_spec(-1), kv_spec(0), kv_spec(1), kv_spec(-1), kv_spec(0), kv_spec(1),
        ],
        out_specs=pl.BlockSpec((ATT_TQ, Q_DIM), lambda b, i: (b * nq + i, 0)),
        out_shape=jax.ShapeDtypeStruct((t, Q_DIM), BF16),
        compiler_params=_cparams(("parallel", "arbitrary")),
        name="attn_core",
    )(sink, q, k2, k2, k2, v2, v2, v2)


def _chunk_cumsum(g, rev):
    c, w = g.shape
    sub = lax.broadcasted_iota(jnp.int32, (SUBLANES, w), 0)
    groups = [g[i * SUBLANES:(i + 1) * SUBLANES] for i in range(c // SUBLANES)]
    sh = 1
    while sh < SUBLANES:
        if rev:
            groups = [x + jnp.where(sub < SUBLANES - sh, pltpu.roll(x, SUBLANES - sh, 0), 0.0) for x in groups]
        else:
            groups = [x + jnp.where(sub >= sh, pltpu.roll(x, sh, 0), 0.0) for x in groups]
        sh *= 2
    order = range(len(groups) - 1, -1, -1) if rev else range(len(groups))
    run = None
    out = [None] * len(groups)
    for i in order:
        gi = groups[i] if run is None else groups[i] + run
        out[i] = gi
        run = jnp.broadcast_to(gi[0:1] if rev else gi[SUBLANES - 1:SUBLANES], (SUBLANES, w))
    return jnp.concatenate(out, axis=0)


def _hgrn_in_kernel(layer, x_ref, w_ref, lg_ref, q_ref, kf_ref, bf_ref, kb_ref, bb_ref, v_ref, gt_ref):
    for r0 in range(0, x_ref.shape[0], HGRN_IN_TR):
        _hgrn_in_rows(layer, slice(r0, r0 + HGRN_IN_TR), x_ref, w_ref, lg_ref,
                      q_ref, kf_ref, bf_ref, kb_ref, bb_ref, v_ref, gt_ref)


def _hgrn_in_rows(layer, rws, x_ref, w_ref, lg_ref, q_ref, kf_ref, bf_ref, kb_ref, bb_ref, v_ref, gt_ref):
    xb = x_ref[rws, :].astype(BF16)
    tm = xb.shape[0]
    tn = HGRN_IN_TN

    def proj(sec, cols):
        lo = sec * D_MODEL + cols.start
        return jnp.dot(xb, w_ref[:, lo:lo + tn], preferred_element_type=F32)

    def silu(z):
        return z * _sigmoid(z)

    def put(ref, cols, val, row0=0):
        r = slice(rws.start + row0, rws.start + row0 + val.shape[0])
        for lo in range(cols.start, cols.stop, SCAN_PAIR):
            ref[lo // SCAN_PAIR, r, :] = val[:, lo - cols.start:lo - cols.start + SCAN_PAIR]

    def lower_bound(d, cols):
        rows = [lg_ref[2 * dd + d:2 * dd + d + 1, cols] for dd in range(DEPTH)]
        m = functools.reduce(jnp.maximum, rows)
        e = [jnp.exp(r - m) for r in rows]
        return functools.reduce(jnp.add, e[1:layer + 1]) / functools.reduce(jnp.add, e)

    def forget(sec, d, cols, k_ref, b_ref):
        lb = lower_bound(d, cols)
        c = 0.5 * (1.0 - lb)
        t = c * jnp.tanh(0.5 * proj(sec, cols))
        put(k_ref, cols, (c - t).astype(BF16))
        g = jnp.log2((0.5 * (1.0 + lb)) + t)
        for ci in range(tm // SCAN_C):
            rows = slice(ci * SCAN_C, (ci + 1) * SCAN_C)
            put(b_ref, cols, _chunk_cumsum(g[rows], d == 1), rows.start)

    for cc in range(D_MODEL // tn):
        cols = slice(cc * tn, (cc + 1) * tn)
        put(q_ref, cols, silu(proj(0, cols)).astype(BF16))
        forget(1, 0, cols, kf_ref, bf_ref)
        forget(2, 1, cols, kb_ref, bb_ref)
        put(v_ref, cols, proj(3, cols).astype(BF16))
        gt_ref[rws, cols] = silu(proj(4, cols)).astype(BF16)


def _hgrn_in(x, w, logits, layer, tm):
    t = x.shape[0]
    npairs = D_MODEL // SCAN_PAIR
    row = pl.BlockSpec((tm, D_MODEL), lambda i: (i, 0))
    prow = pl.BlockSpec((npairs, tm, SCAN_PAIR), lambda i: (0, i, 0))
    pbf = jax.ShapeDtypeStruct((npairs, t, SCAN_PAIR), BF16)
    pf32 = jax.ShapeDtypeStruct((npairs, t, SCAN_PAIR), F32)
    return pl.pallas_call(
        functools.partial(_hgrn_in_kernel, layer),
        grid=(t // tm,),
        in_specs=[row, _resident((D_MODEL, 5 * D_MODEL)), _resident((2 * DEPTH, D_MODEL))],
        out_specs=[prow, prow, prow, prow, prow, prow, row],
        out_shape=[pbf, pbf, pf32, pbf, pf32, pbf, jax.ShapeDtypeStruct((t, D_MODEL), BF16)],
        compiler_params=_cparams(("parallel",)),
        name="hgrn_in",
    )(x, w, logits)


def _block_diag(a0, a1):
    z = jnp.zeros_like(a0)
    return jnp.concatenate([jnp.concatenate([a0, z], axis=1), jnp.concatenate([z, a1], axis=1)], axis=0)


def _scan_consts(rev):
    c = SCAN_C
    row = lax.broadcasted_iota(jnp.int32, (c, 2 * c), 0)
    col = lax.broadcasted_iota(jnp.int32, (c, 2 * c), 1) % c
    rowk = lax.broadcasted_iota(jnp.int32, (c, SCAN_PAIR), 0)
    masks, signs = {}, {}
    half = c // 2
    while half >= SCAN_BASE:
        size = 2 * half
        q_row = ((row % size) < half) if rev else ((row % size) >= half)
        k_col = ((col % size) >= half) if rev else ((col % size) < half)
        masks[size] = ((row // size) == (col // size)) & q_row & k_col
        q_rowk = ((rowk % size) < half) if rev else ((rowk % size) >= half)
        signs[size] = jnp.where(q_rowk, 1.0, -1.0).astype(F32)
        half //= 2
    masks[0] = ((row // SCAN_BASE) == (col // SCAN_BASE)) & ((col >= row) if rev else (col <= row))
    return masks, signs


def _scan_chunk_local(q, k, v, b, rev, masks, signs):
    c = SCAN_C
    hk = HGRN_KEY
    btot = b[0:1] if rev else b[c - 1:c]

    def ref_rows(size, offset):
        parts = [jnp.broadcast_to(b[p * size + offset:p * size + offset + 1], (size, SCAN_PAIR))
                 for p in range(c // size)]
        return parts[0] if len(parts) == 1 else jnp.concatenate(parts, axis=0)

    def pair_scores(qs, ks):
        kbd = _block_diag(ks[:, :hk], ks[:, hk:])
        return lax.dot_general(qs, kbd, (((1,), (1,)), ((), ())), preferred_element_type=F32)

    scores = None
    half = c // 2
    while half >= SCAN_BASE:
        size = 2 * half
        e = jnp.exp2((b - ref_rows(size, half if rev else half - 1)) * signs[size]).astype(BF16)
        s = jnp.where(masks[size], pair_scores(q * e, k * e), 0.0)
        scores = s if scores is None else scores + s
        half //= 2
    d = b - ref_rows(SCAN_BASE, SCAN_BASE // 2)
    s = pair_scores(q * jnp.exp2(d).astype(BF16), k * jnp.exp2(-d).astype(BF16))
    scores = scores + jnp.where(masks[0], s, 0.0)

    ke = k * jnp.exp2(btot - b).astype(BF16)
    contrib = [lax.dot_general(v[:, sl], ke[:, sl], (((0,), (0,)), ((), ())), preferred_element_type=F32)
               for sl in (slice(0, hk), slice(hk, 2 * hk))]
    return q * jnp.exp2(b).astype(BF16), scores.astype(BF16), contrib, jnp.exp2(btot)


def _scan_local(q_ref, k_ref, v_ref, b_ref, rev, base, nchunks):
    masks, signs = _scan_consts(rev)
    rows = [pl.ds(base + ci * SCAN_C, SCAN_C) for ci in range(nchunks)]
    return rows, [_scan_chunk_local(q_ref[r, :], k_ref[r, :], v_ref[r, :], b_ref[r, :], rev, masks, signs)
                  for r in rows]


def _scan_serial(rows, local, v_ref, o_ref, st_ref, rev):
    hk = HGRN_KEY
    nchunks = len(rows)
    st = [st_ref[0], st_ref[1]]
    for ci in (range(nchunks - 1, -1, -1) if rev else range(nchunks)):
        qe, scores, contrib, dec = local[ci]
        v = v_ref[rows[ci], :]
        w = _block_diag(st[0].astype(BF16), st[1].astype(BF16))
        o_ref[rows[ci], :] = (
            jnp.dot(scores, _block_diag(v[:, :hk], v[:, hk:]), preferred_element_type=F32)
            + lax.dot_general(qe, w, (((1,), (1,)), ((), ())), preferred_element_type=F32)
        ).astype(o_ref.dtype)
        st = [st[0] * dec[:, :hk] + contrib[0], st[1] * dec[:, hk:] + contrib[1]]
    st_ref[0] = st[0]
    st_ref[1] = st[1]


def _scan_kernel(nsub, sub, qf_ref, qb_ref, vf_ref, vb_ref, kf_ref, bf_ref, kb_ref, bb_ref,
                 of_ref, ob_ref, stf_ref, stb_ref):
    @pl.when(pl.program_id(2) == 0)
    def _():
        stf_ref[...] = jnp.zeros_like(stf_ref)
        stb_ref[...] = jnp.zeros_like(stb_ref)

    nchunks = sub // SCAN_C

    def body(i, carry):
        fbase = pl.multiple_of(i * sub, sub)
        bbase = pl.multiple_of((nsub - 1 - i) * sub, sub)
        rows_f, local_f = _scan_local(qf_ref, kf_ref, vf_ref, bf_ref, False, fbase, nchunks)
        rows_b, local_b = _scan_local(qb_ref, kb_ref, vb_ref, bb_ref, True, bbase, nchunks)
        _scan_serial(rows_f, local_f, vf_ref, of_ref, stf_ref, False)
        _scan_serial(rows_b, local_b, vb_ref, ob_ref, stb_ref, True)
        return carry

    lax.fori_loop(0, nsub, body, 0)


def _hgrn_scan(q, kf, bf, kb, bb, v, batch, seq, blk, sub):
    npairs, t, _ = q.shape
    nb = seq // blk
    fwd = pl.BlockSpec((None, blk, SCAN_PAIR), lambda b, h, n: (h, b * nb + n, 0))
    bwd = pl.BlockSpec((None, blk, SCAN_PAIR), lambda b, h, n: (h, b * nb + nb - 1 - n, 0))
    out = jax.ShapeDtypeStruct((npairs, t, SCAN_PAIR), BF16)
    state = pltpu.VMEM((2, HGRN_KEY, HGRN_KEY), F32)
    return pl.pallas_call(
        functools.partial(_scan_kernel, blk // sub, sub),
        grid=(batch, npairs, nb),
        in_specs=[fwd, bwd, fwd, bwd, fwd, fwd, bwd, bwd],
        out_specs=[fwd, bwd],
        out_shape=[out, out],
        scratch_shapes=[state, state],
        compiler_params=_cparams(("parallel", "parallel", "arbitrary")),
        name="hgrn_scan",
    )(q, q, v, v, kf, bf, kb, bb)


def _ffn_tail(x, p_ref, wi_ref, wo_ref, g_ref, b_ref, wg_ref, wp_ref, out_ref):
    xb = x.astype(BF16)
    acc = None
    for c in range(D_FF // FFN_FC):
        lo = c * FFN_FC
        gate = jnp.dot(xb, wi_ref[:, lo:lo + FFN_FC], preferred_element_type=F32)
        up = jnp.dot(xb, wi_ref[:, D_FF + lo:D_FF + lo + FFN_FC], preferred_element_type=F32)
        h = (gate * _sigmoid(gate) * up).astype(BF16)
        y = jnp.dot(h, wo_ref[lo:lo + FFN_FC, :], preferred_element_type=F32)
        acc = y if acc is None else acc + y
    x2 = _layer_norm(ALPHA * x + acc, g_ref[...], b_ref[...])
    egate = _sigmoid(jnp.dot(x2.astype(BF16), wg_ref[...], preferred_element_type=F32))
    proj = jnp.dot(p_ref[...].astype(BF16), wp_ref[...], preferred_element_type=F32)
    out_ref[...] = x2 + egate * proj


def _attn_tail_kernel(x_ref, o_ref, wm_ref, gm_ref, bm_ref, *rest):
    y = ALPHA * x_ref[...] + jnp.dot(o_ref[...], wm_ref[...], preferred_element_type=F32)
    _ffn_tail(_layer_norm(y, gm_ref[...], bm_ref[...]), *rest)


def _hgrn_tail_kernel(x_ref, of_ref, ob_ref, gt_ref, ng_ref, wm_ref, gm_ref, bm_ref, *rest):
    ys = []
    for h in range(HGRN_HEADS):
        sl = slice(h * HGRN_KEY, (h + 1) * HGRN_KEY)
        ps = slice((h % 2) * HGRN_KEY, (h % 2 + 1) * HGRN_KEY)
        o = of_ref[h // 2, :, ps].astype(F32) + ob_ref[h // 2, :, ps].astype(F32)
        ms = jnp.mean(o * o, axis=-1, keepdims=True)
        ys.append((o * lax.rsqrt(ms + LN_EPS) * ng_ref[...] * gt_ref[:, sl].astype(F32)).astype(BF16))
    y = ALPHA * x_ref[...] + jnp.dot(jnp.concatenate(ys, axis=1), wm_ref[...], preferred_element_type=F32)
    _ffn_tail(_layer_norm(y, gm_ref[...], bm_ref[...]), *rest)


def _layer_tail(kernel_fn, name, x, mixer_inputs, mixer_specs, p, layer, w_in, w_out, g, b, w_gate, w_proj, tm):
    t = x.shape[0]
    row = pl.BlockSpec((tm, D_MODEL), lambda i: (i, 0))
    return pl.pallas_call(
        kernel_fn,
        grid=(t // tm,),
        in_specs=[row] + mixer_specs + [
            pl.BlockSpec((None, tm, PLE_DIM), lambda i: (layer, i, 0)),
            _resident((D_MODEL, 2 * D_FF)), _resident((D_FF, D_MODEL)),
            _resident((1, D_MODEL)), _resident((1, D_MODEL)),
            _resident((D_MODEL, D_MODEL)), _resident((PLE_DIM, D_MODEL))],
        out_specs=row,
        out_shape=jax.ShapeDtypeStruct((t, D_MODEL), F32),
        compiler_params=_cparams(("parallel",)),
        name=name,
    )(x, *mixer_inputs, p, w_in, w_out, g, b, w_gate, w_proj)


def _rope_tables(seq):
    half = ROPE_DIM // 2
    inv = ROPE_THETA ** (-jnp.arange(0, ROPE_DIM, 2, dtype=F32) / ROPE_DIM)
    ang = jnp.arange(seq, dtype=F32)[:, None] * inv[None, :]
    cos, sin = jnp.cos(ang), jnp.sin(ang)
    ones = jnp.ones((seq, HEAD_DIM - ROPE_DIM), F32)
    zeros = jnp.zeros((seq, half), F32)
    rest = jnp.zeros((seq, HEAD_DIM - ROPE_DIM), F32)
    reps = LANES // HEAD_DIM
    ctab = jnp.tile(jnp.concatenate([cos, cos, ones], axis=1), (1, reps))
    satab = jnp.tile(jnp.concatenate([zeros, sin, rest], axis=1), (1, reps))
    sbtab = jnp.tile(jnp.concatenate([-sin, zeros, rest], axis=1), (1, reps))
    return ctab, satab, sbtab


def _twice_per_head(w):
    d = w.shape[0]
    w = w.reshape(d, N_KV_HEADS, 1, HEAD_DIM)
    return jnp.broadcast_to(w, (d, N_KV_HEADS, 2, HEAD_DIM)).reshape(d, KV2_DIM)


def _tile(n, pref):
    while n % pref:
        pref //= 2
    return pref


def kernel(x, p, att_w_qkv, att_sink, att_w_o, hgrn_w_in, hgrn_lb_logits, hgrn_norm_g, hgrn_w_o,
           ln_mix_g, ln_mix_b, ffn_w_in, ffn_w_out, ln_ffn_g, ln_ffn_b, ple_w_gate, ple_w_proj):
    batch, seq, _ = x.shape
    t = batch * seq
    assert seq % ATT_TQ == 0
    tm = _tile(seq, ROW_TILE)
    blk = _tile(seq, SCAN_BLK)
    sub = _tile(blk, SCAN_SUB)
    assert tm % SCAN_C == 0 and sub % SCAN_C == 0
    row = pl.BlockSpec((tm, D_MODEL), lambda r: (r, 0))
    wres, vres = _resident((D_MODEL, D_MODEL)), _resident((1, D_MODEL))
    ctab, satab, sbtab = _rope_tables(seq)
    logits = hgrn_lb_logits.astype(F32).reshape(2 * DEPTH, D_MODEL)
    h = x.reshape(t, D_MODEL)
    pf = p.reshape(DEPTH, t, PLE_DIM)
    vec = lambda a: a.reshape(1, -1).astype(F32)
    for i in range(DEPTH):
        j = i // 2
        if i % 2 == 0:
            wq, wk, wv = (att_w_qkv[j][:, :Q_DIM], att_w_qkv[j][:, Q_DIM:Q_DIM + KV_DIM],
                          att_w_qkv[j][:, Q_DIM + KV_DIM:])
            w = jnp.concatenate([wq, _twice_per_head(wk), _twice_per_head(wv)], axis=1).astype(BF16)
            q, k2, v2 = _qkv_proj(h, w, ctab, satab, sbtab, seq, tm)
            o = _attention(q, k2, v2, att_sink[j].astype(F32) * LOG2E, batch, seq)
            tail = (_attn_tail_kernel, "attn_tail")
            mixer_inputs = [o, att_w_o[j].astype(BF16), vec(ln_mix_g[i]), vec(ln_mix_b[i])]
            mixer_specs = [row, wres, vres, vres]
        else:
            q, kf, bf, kb, bb, v, gate = _hgrn_in(h, hgrn_w_in[j].astype(BF16), logits, i, tm)
            of, ob = _hgrn_scan(q, kf, bf, kb, bb, v, batch, seq, blk, sub)
            tail = (_hgrn_tail_kernel, "hgrn_tail")
            mixer_inputs = [of, ob, gate, vec(hgrn_norm_g[j]), hgrn_w_o[j].astype(BF16),
                            vec(ln_mix_g[i]), vec(ln_mix_b[i])]
            prow = pl.BlockSpec((D_MODEL // SCAN_PAIR, tm, SCAN_PAIR), lambda r: (0, r, 0))
            mixer_specs = [prow, prow, row, _resident((1, HGRN_KEY)), wres, vres, vres]
        h = _layer_tail(*tail, h, mixer_inputs, mixer_specs, pf, i,
                        ffn_w_in[i].astype(BF16), ffn_w_out[i].astype(BF16),
                        vec(ln_ffn_g[i]), vec(ln_ffn_b[i]),
                        ple_w_gate[i].astype(BF16), ple_w_proj[i].astype(BF16), tm)
    return h.reshape(batch, seq, D_MODEL)
```
